```python
import jax, jax.numpy as jnp
from jax import lax
import numpy as np

D_MODEL = 1024
BATCH = 8
SEQ = 2048
DEPTH = 4
DEC_BATCH = 32
DEC_SEQ = 8
PAST_LEN = 8192
PAGE_SIZE = 128

N_MIXERS = 3
D_INNER = 2 * D_MODEL
H_A = 8
DK_A = D_INNER // (2 * H_A)
DV_A = D_INNER // H_A
CHUNK_A = 128
CONV_W = 3
DH_C = 128
H_C = D_INNER // DH_C
Q_BLOCK = 128
PE_DIM = 256
EPS = 1e-6
FOX_BIAS_LO = 4.0
FOX_BIAS_HI = 9.0
N_LAYERS_A = (DEPTH + 2) // 3
N_LAYERS_B = (DEPTH + 1) // 3
N_LAYERS_C = DEPTH // 3
A_COLS = 2 * H_A * DK_A + D_INNER + 2 * H_A + 2 * D_INNER
B_COLS = 4 * D_INNER
C_COLS = 4 * D_INNER + H_C

kernel_name = 'hybrid_mlstm_conv_fox_step'


def rmsnorm(x, g):
    xf = x.astype(jnp.float32)
    y = xf * lax.rsqrt(jnp.mean(xf * xf, axis=-1, keepdims=True) + EPS)
    return (y * g.astype(jnp.float32)).astype(x.dtype)


def per_layer_embed(x, p_i, w_pe, w_pg, g):
    gate = jax.nn.sigmoid((rmsnorm(x, g) @ w_pg).astype(jnp.float32))
    return x + ((p_i @ w_pe).astype(jnp.float32) * gate).astype(x.dtype)


def mlstm_project(h, w_in, b_i, b_f):
    bsz, t = h.shape[0], h.shape[1]
    qk = H_A * DK_A
    cuts = [qk, 2 * qk, 2 * qk + D_INNER, 2 * qk + D_INNER + H_A,
            2 * qk + D_INNER + 2 * H_A, 2 * qk + 2 * D_INNER + 2 * H_A]
    q, k, v, ig, fg, o, z = jnp.split(h @ w_in, cuts, axis=-1)
    q = q.reshape(bsz, t, H_A, DK_A).astype(jnp.float32) * (DK_A ** -0.5)
    k = k.reshape(bsz, t, H_A, DK_A).astype(jnp.float32)
    v = v.reshape(bsz, t, H_A, DV_A).astype(jnp.float32)
    li = (ig + b_i).astype(jnp.float32)
    lf = jax.nn.log_sigmoid((fg + b_f).astype(jnp.float32))
    return q, k, v, li, lf, o, z


def mlstm_chunk(carry, xs):
    c0, n0, m0 = carry
    q, k, v, li, lf = xs
    L = q.shape[1]
    b = jnp.cumsum(lf, axis=1)
    a = b + m0[:, None, :]
    d = b[:, :, None, :] - b[:, None, :, :] + li[:, None, :, :]
    causal = jnp.tril(jnp.ones((L, L), dtype=bool))
    d = jnp.where(causal[None, :, :, None], d, -jnp.inf)
    m_t = jnp.maximum(a, jnp.max(d, axis=2))
    w_intra = jnp.exp(d - m_t[:, :, None, :])
    w_inter = jnp.exp(a - m_t)
    s = jnp.einsum('bthd,bshd->btsh', q, k) * w_intra
    num = jnp.einsum('btsh,bshv->bthv', s, v) + w_inter[..., None] * jnp.einsum('bhvd,bthd->bthv', c0, q)
    den = jnp.sum(s, axis=2) + w_inter * jnp.einsum('bhd,bthd->bth', n0, q)
    h = num / jnp.maximum(jnp.abs(den), jnp.exp(-m_t))[..., None]
    b_last = b[:, -1, :]
    g = b_last[:, None, :] - b + li
    m_new = jnp.maximum(b_last + m0, jnp.max(g, axis=1))
    ws = jnp.exp(g - m_new[:, None, :])
    decay = jnp.exp(b_last + m0 - m_new)
    c_new = decay[..., None, None] * c0 + jnp.einsum('bsh,bshv,bshd->bhvd', ws, v, k)
    n_new = decay[..., None] * n0 + jnp.einsum('bsh,bshd->bhd', ws, k)
    return (c_new, n_new, m_new), h


def mlstm_mixer(h, w_in, b_i, b_f, hnorm, w_out, c0, n0, m0):
    bsz, t = h.shape[0], h.shape[1]
    q, k, v, li, lf, o, z = mlstm_project(h, w_in, b_i, b_f)
    chunk = min(CHUNK_A, t)
    nc = t // chunk

    def to_chunks(a):
        return jnp.moveaxis(a.reshape((bsz, nc, chunk) + a.shape[2:]), 1, 0)

    carry = (c0.astype(jnp.float32), n0.astype(jnp.float32), m0.astype(jnp.float32))
    carry, hc = lax.scan(mlstm_chunk, carry,
                         (to_chunks(q), to_chunks(k), to_chunks(v), to_chunks(li), to_chunks(lf)))
    hh = jnp.moveaxis(hc, 0, 1).reshape(bsz, t, H_A, DV_A)
    hh = hh * lax.rsqrt(jnp.mean(hh * hh, axis=-1, keepdims=True) + EPS)
    hh = hh.reshape(bsz, t, D_INNER) * hnorm.astype(jnp.float32)
    y = hh * jax.nn.sigmoid(o.astype(jnp.float32)) * jax.nn.silu(z.astype(jnp.float32))
    return y.astype(h.dtype) @ w_out, carry


def conv_mixer(h, w_in, conv_w, w_out, conv_state):
    bg, cg, xin, z = jnp.split(h @ w_in, 4, axis=-1)
    u = cg * xin
    t = u.shape[1]
    u_pad = jnp.concatenate([conv_state.astype(u.dtype), u], axis=1)
    conv = sum(conv_w[j] * u_pad[:, j:j + t] for j in range(CONV_W))
    y = bg * conv * jax.nn.silu(z)
    return y @ w_out, u_pad[:, -(CONV_W - 1):]


def fox_project(h, w_in, b_f):
    bsz, t = h.shape[0], h.shape[1]
    cuts = [D_INNER, 2 * D_INNER, 3 * D_INNER, 3 * D_INNER + H_C]
    q, k, v, fg, z = jnp.split(h @ w_in, cuts, axis=-1)
    q = q.reshape(bsz, t, H_C, DH_C) * (DH_C ** -0.5)
    k = k.reshape(bsz, t, H_C, DH_C)
    v = v.reshape(bsz, t, H_C, DH_C)
    lf = jax.nn.log_sigmoid((fg + b_f).astype(jnp.float32))
    return q, k, v, lf, z


def fox_prompt(q, k, v, lf):
    bsz, t = q.shape[0], q.shape[1]
    c = jnp.cumsum(lf, axis=1)
    nb = t // Q_BLOCK
    qb = jnp.moveaxis(q.reshape(bsz, nb, Q_BLOCK, H_C, DH_C), 1, 0)
    cb = jnp.moveaxis(c.reshape(bsz, nb, Q_BLOCK, H_C), 1, 0)
    qpos = jnp.arange(t, dtype=jnp.int32).reshape(nb, Q_BLOCK)
    kpos = jnp.arange(t, dtype=jnp.int32)
    c_key = jnp.transpose(c, (0, 2, 1))[:, :, None, :]

    def block(args):
        qi, ci, pi = args
        s = jnp.einsum('bthd,bshd->bhts', qi, k).astype(jnp.float32)
        s = s + jnp.transpose(ci, (0, 2, 1))[..., None] - c_key
        s = jnp.where(pi[:, None] >= kpos[None, :], s, -jnp.inf)
        p = jax.nn.softmax(s, axis=-1)
        return jnp.einsum('bhts,bshd->bthd', p.astype(v.dtype), v)

    o = lax.map(block, (qb, cb, qpos))
    return jnp.moveaxis(o, 0, 1).reshape(bsz, t, H_C * DH_C)


def fox_sample(q, k, v, lf, pool_k, pool_v, pool_lf, layer, page_table):
    bsz, t = q.shape[0], q.shape[1]
    n_pages = page_table.shape[1]
    lf_past = pool_lf[layer, page_table].reshape(bsz, n_pages * PAGE_SIZE, H_C).astype(jnp.float32)
    suffix = lax.cumsum(lf_past, axis=1, reverse=True) - lf_past
    c_new = jnp.cumsum(lf, axis=1)
    c_q = jnp.transpose(c_new, (0, 2, 1))[..., None]
    suf_pages = jnp.moveaxis(suffix.reshape(bsz, n_pages, PAGE_SIZE, H_C), 1, 0)
    qf = q.astype(jnp.float32)

    def page_step(carry, xs):
        m, l, acc = carry
        ids, suf = xs
        kp = pool_k[layer, ids].astype(jnp.float32)
        vp = pool_v[layer, ids].astype(jnp.float32)
        s = jnp.einsum('bthd,bphd->bhtp', qf, kp) + jnp.transpose(suf, (0, 2, 1))[:, :, None, :] + c_q
        m_new = jnp.maximum(m, jnp.max(s, axis=-1))
        corr = jnp.exp(m - m_new)
        p = jnp.exp(s - m_new[..., None])
        l = l * corr + jnp.sum(p, axis=-1)
        acc = acc * corr[..., None] + jnp.einsum('bhtp,bphd->bhtd', p, vp)
        return (m_new, l, acc), None

    init = (jnp.full((bsz, H_C, t), -jnp.inf, jnp.float32),
            jnp.zeros((bsz, H_C, t), jnp.float32),
            jnp.zeros((bsz, H_C, t, DH_C), jnp.float32))
    (m, l, acc), _ = lax.scan(page_step, init, (jnp.transpose(page_table), suf_pages))
    s_new = jnp.einsum('bthd,bshd->bhts', qf, k.astype(jnp.float32)) + c_q - jnp.transpose(c_new, (0, 2, 1))[:, :, None, :]
    causal = jnp.tril(jnp.ones((t, t), dtype=bool))
    s_new = jnp.where(causal[None, None], s_new, -jnp.inf)
    m_f = jnp.maximum(m, jnp.max(s_new, axis=-1))
    corr = jnp.exp(m - m_f)
    p = jnp.exp(s_new - m_f[..., None])
    l = l * corr + jnp.sum(p, axis=-1)
    acc = acc * corr[..., None] + jnp.einsum('bhts,bshd->bhtd', p, v.astype(jnp.float32))
    o = acc / l[..., None]
    return jnp.transpose(o, (0, 2, 1, 3)).reshape(bsz, t, H_C * DH_C).astype(q.dtype)


def setup_inputs(seed: int = 0) -> dict:
    key = jax.random.key(seed)
    ks = iter(jax.random.split(key, 40))

    def nrm(shape, scale=1.0):
        return scale * jax.random.normal(next(ks), shape, jnp.float32)

    n_pages = PAST_LEN // PAGE_SIZE
    n_used = DEC_BATCH * n_pages
    n_pool = (n_used * 5) // 4
    page_table = jax.random.permutation(next(ks), n_pool)[:n_used].reshape(DEC_BATCH, n_pages).astype(jnp.int32)
    fox_bias = jnp.linspace(FOX_BIAS_LO, FOX_BIAS_HI, H_C, dtype=jnp.float32)
    return {
        'x_prompt': nrm((BATCH, SEQ, D_MODEL)),
        'x_sample': nrm((DEC_BATCH, DEC_SEQ, D_MODEL)),
        'state_mlstm_c': nrm((N_LAYERS_A, DEC_BATCH, H_A, DV_A, DK_A), 0.1),
        'state_mlstm_n': nrm((N_LAYERS_A, DEC_BATCH, H_A, DK_A), 0.1),
        'state_mlstm_m': nrm((N_LAYERS_A, DEC_BATCH, H_A)),
        'state_conv': nrm((N_LAYERS_B, DEC_BATCH, CONV_W - 1, D_INNER)),
        'cache_k': nrm((N_LAYERS_C, n_pool, PAGE_SIZE, H_C, DH_C)),
        'cache_v': nrm((N_LAYERS_C, n_pool, PAGE_SIZE, H_C, DH_C)),
        'cache_logf': jax.nn.log_sigmoid(fox_bias + nrm((N_LAYERS_C, n_pool, PAGE_SIZE, H_C), 0.5)),
        'page_table': page_table,
        'p_prompt': nrm((DEPTH, BATCH, SEQ, PE_DIM)),
        'p_sample': nrm((DEPTH, DEC_BATCH, DEC_SEQ, PE_DIM)),
        'ln_gain': 1.0 + nrm((DEPTH, D_MODEL), 0.05),
        'w_in_a': nrm((N_LAYERS_A, D_MODEL, A_COLS), D_MODEL ** -0.5),
        'b_igate_a': nrm((N_LAYERS_A, H_A), 0.1),
        'b_fgate_a': 3.0 + nrm((N_LAYERS_A, H_A), 0.5),
        'hnorm_a': 1.0 + nrm((N_LAYERS_A, D_INNER), 0.05),
        'w_out_a': nrm((N_LAYERS_A, D_INNER, D_MODEL), D_INNER ** -0.5),
        'w_in_b': nrm((N_LAYERS_B, D_MODEL, B_COLS), D_MODEL ** -0.5),
        'conv_w_b': nrm((N_LAYERS_B, CONV_W, D_INNER), CONV_W ** -0.5),
        'w_out_b': nrm((N_LAYERS_B, D_INNER, D_MODEL), D_INNER ** -0.5),
        'w_in_c': nrm((N_LAYERS_C, D_MODEL, C_COLS), D_MODEL ** -0.5),
        'b_fgate_c': fox_bias + nrm((N_LAYERS_C, H_C), 0.3),
        'w_out_c': nrm((N_LAYERS_C, D_INNER, D_MODEL), D_INNER ** -0.5),
        'w_pe': nrm((DEPTH, PE_DIM, D_MODEL), 0.5 * PE_DIM ** -0.5),
        'w_pg': nrm((DEPTH, D_MODEL, D_MODEL), D_MODEL ** -0.5),
        'pe_norm': 1.0 + nrm((DEPTH, D_MODEL), 0.05),
        'final_norm': 1.0 + nrm((D_MODEL,), 0.05),
    }


def reference(x_prompt, x_sample, state_mlstm_c, state_mlstm_n, state_mlstm_m, state_conv,
              cache_k, cache_v, cache_logf, page_table, p_prompt, p_sample,
              ln_gain, w_in_a, b_igate_a, b_fgate_a, hnorm_a, w_out_a,
              w_in_b, conv_w_b, w_out_b, w_in_c, b_fgate_c, w_out_c,
              w_pe, w_pg, pe_norm, final_norm):
    xp, xs = x_prompt, x_sample
    bp = xp.shape[0]
    mc_p, mn_p, mm_p, mc_s, mn_s, mm_s = [], [], [], [], [], []
    cv_p, cv_s = [], []
    kp_l, vp_l, fp_l, ks_l, vs_l, fs_l = [], [], [], [], [], []
    for i in range(DEPTH):
        j = i // N_MIXERS
        hp = rmsnorm(xp, ln_gain[i])
        hs = rmsnorm(xs, ln_gain[i])
        if i % N_MIXERS == 0:
            yp, (c_p, n_p, m_p) = mlstm_mixer(
                hp, w_in_a[j], b_igate_a[j], b_fgate_a[j], hnorm_a[j], w_out_a[j],
                jnp.zeros((bp, H_A, DV_A, DK_A), jnp.float32),
                jnp.zeros((bp, H_A, DK_A), jnp.float32),
                jnp.zeros((bp, H_A), jnp.float32))
            ys, (c_s, n_s, m_s) = mlstm_mixer(
                hs, w_in_a[j], b_igate_a[j], b_fgate_a[j], hnorm_a[j], w_out_a[j],
                state_mlstm_c[j], state_mlstm_n[j], state_mlstm_m[j])
            mc_p.append(c_p); mn_p.append(n_p); mm_p.append(m_p)
            mc_s.append(c_s); mn_s.append(n_s); mm_s.append(m_s)
        elif i % N_MIXERS == 1:
            yp, st_p = conv_mixer(hp, w_in_b[j], conv_w_b[j], w_out_b[j],
                                  jnp.zeros((bp, CONV_W - 1, D_INNER), hp.dtype))
            ys, st_s = conv_mixer(hs, w_in_b[j], conv_w_b[j], w_out_b[j], state_conv[j])
            cv_p.append(st_p); cv_s.append(st_s)
        else:
            q, k, v, lf, z = fox_project(hp, w_in_c[j], b_fgate_c[j])
            yp = (fox_prompt(q, k, v, lf) * jax.nn.silu(z)) @ w_out_c[j]
            kp_l.append(k); vp_l.append(v); fp_l.append(lf)
            q2, k2, v2, lf2, z2 = fox_project(hs, w_in_c[j], b_fgate_c[j])
            ys = (fox_sample(q2, k2, v2, lf2, cache_k, cache_v, cache_logf, j, page_table)
                  * jax.nn.silu(z2)) @ w_out_c[j]
            ks_l.append(k2); vs_l.append(v2); fs_l.append(lf2)
        xp = xp + yp
        xs = xs + ys
        xp = per_layer_embed(xp, p_prompt[i], w_pe[i], w_pg[i], pe_norm[i])
        xs = per_layer_embed(xs, p_sample[i], w_pe[i], w_pg[i], pe_norm[i])
    y_prompt = rmsnorm(xp, final_norm)
    y_sample = rmsnorm(xs, final_norm)
    return (y_prompt, y_sample,
            jnp.stack(mc_p), jnp.stack(mn_p), jnp.stack(mm_p),
            jnp.stack(mc_s), jnp.stack(mn_s), jnp.stack(mm_s),
            jnp.stack(cv_p), jnp.stack(cv_s),
            jnp.stack(kp_l), jnp.stack(vp_l), jnp.stack(fp_l),
            jnp.stack(ks_l), jnp.stack(vs_l), jnp.stack(fs_l))
```

```python
import functools

import jax
import jax.numpy as jnp
from jax import lax
from jax.experimental import pallas as pl
from jax.experimental.pallas import tpu as pltpu

F32 = jnp.float32
BF16 = jnp.bfloat16

EPS = 1e-6
DEPTH = 4
N_MIXERS = 3
D_MODEL = 1024
D_INNER = 2 * D_MODEL
H_A = 8
DK_A = D_INNER // (2 * H_A)
DV_A = D_INNER // H_A
CONV_W = 3
DH_C = 128
H_C = D_INNER // DH_C
PE_DIM = 256
PAGE_SIZE = 128

LANES = 128
SUBLANES = 8
CHUNK = 128
VMEM_LIMIT_BYTES = 48 * 1024 * 1024


def _cparams(n_axes):
    return pltpu.CompilerParams(
        dimension_semantics=("arbitrary",) * n_axes,
        vmem_limit_bytes=VMEM_LIMIT_BYTES)


def _log_sigmoid(x):
    return jnp.minimum(x, 0.0) - jnp.log1p(jnp.exp(-jnp.abs(x)))


def _silu(x):
    return x * jax.nn.sigmoid(x)


def _rms(x, gain):
    return x * lax.rsqrt(jnp.mean(x * x, axis=-1, keepdims=True) + EPS) * gain


def _split3(x):
    hi = x.astype(BF16)
    r1 = x - hi.astype(F32)
    mid = r1.astype(BF16)
    lo = (r1 - mid.astype(F32)).astype(BF16)
    return hi, mid, lo


def _pad_rows(x, rows):
    if x.shape[0] == rows:
        return x
    return jnp.concatenate([x, jnp.zeros((rows - x.shape[0], x.shape[1]), x.dtype)], axis=0)


def _proj_kernel(x_ref, g_ref, w_ref, wg_ref, o_ref, og_ref, h_scr):
    @pl.when(pl.program_id(1) == 0)
    def _():
        h = _rms(x_ref[...], g_ref[...]).astype(BF16)
        h_scr[...] = h
        og_ref[...] = jnp.dot(h, wg_ref[...], preferred_element_type=F32)

    o_ref[...] = jnp.dot(h_scr[...], w_ref[...], preferred_element_type=F32).astype(o_ref.dtype)


def _proj(x, gain, w, wg, out_dtype):
    m, d = x.shape
    n = w.shape[1]
    ng = wg.shape[1]
    tm = min(512, m)
    tn = 1024
    return pl.pallas_call(
        _proj_kernel,
        grid=(m // tm, n // tn),
        in_specs=[
            pl.BlockSpec((tm, d), lambda i, j: (i, 0)),
            pl.BlockSpec((1, d), lambda i, j: (0, 0)),
            pl.BlockSpec((d, tn), lambda i, j: (0, j)),
            pl.BlockSpec((d, ng), lambda i, j: (0, 0)),
        ],
        out_specs=[
            pl.BlockSpec((tm, tn), lambda i, j: (i, j)),
            pl.BlockSpec((tm, ng), lambda i, j: (i, 0)),
        ],
        out_shape=[
            jax.ShapeDtypeStruct((m, n), out_dtype),
            jax.ShapeDtypeStruct((m, ng), F32),
        ],
        scratch_shapes=[pltpu.VMEM((tm, d), BF16)],
        compiler_params=_cparams(2),
        name="norm_proj",
    )(x, gain, w, wg)


def _out_embed_kernel(act_ref, x_ref, p_ref, wo_ref, wpe_ref, wpg_ref, g_ref, fg_ref,
                      xo_ref, *maybe_y_ref):
    y = jnp.dot(act_ref[...].astype(BF16), wo_ref[...], preferred_element_type=F32)
    x1 = x_ref[...] + y
    rn = _rms(x1, g_ref[...]).astype(BF16)
    gate = jax.nn.sigmoid(jnp.dot(rn, wpg_ref[...], preferred_element_type=F32))
    pe = jnp.dot(p_ref[...].astype(BF16), wpe_ref[...], preferred_element_type=F32)
    x2 = x1 + pe * gate
    xo_ref[...] = x2
    if maybe_y_ref:
        maybe_y_ref[0][...] = _rms(x2, fg_ref[...])


def _out_embed(act, x, p, wo, wpe, wpg, gain, final_gain, final):
    m, d = x.shape
    di = act.shape[1]
    pe = p.shape[1]
    tm = min(512, m)
    row = lambda i: (i, 0)
    const = lambda i: (0, 0)
    n_out = 2 if final else 1
    outs = pl.pallas_call(
        _out_embed_kernel,
        grid=(m // tm,),
        in_specs=[
            pl.BlockSpec((tm, di), row),
            pl.BlockSpec((tm, d), row),
            pl.BlockSpec((tm, pe), row),
            pl.BlockSpec((di, d), const),
            pl.BlockSpec((pe, d), const),
            pl.BlockSpec((d, d), const),
            pl.BlockSpec((1, d), const),
            pl.BlockSpec((1, d), const),
        ],
        out_specs=[pl.BlockSpec((tm, d), row)] * n_out,
        out_shape=[jax.ShapeDtypeStruct((m, d), F32)] * n_out,
        compiler_params=_cparams(1),
        name="out_embed",
    )(act, x, p, wo, wpe, wpg, gain, final_gain)
    return outs if final else (outs[0], None)


def _mlstm_kernel(q_ref, k_ref, v_ref, o_ref, z_ref, g_ref, bias_ref, hn_ref,
                  c0_ref, n0_ref, m0_ref,
                  act_ref, c_out, n_out, m_out,
                  ct_scr, n_scr, m_scr, *, lv):
    c = pl.program_id(1)
    L = CHUNK
    scale = DK_A ** -0.5

    @pl.when(c == 0)
    def _():
        for h in range(H_A):
            ct_scr[h] = c0_ref[0, h].T
        n_scr[...] = n0_ref[0]
        m_scr[...] = m0_ref[0]

    q = _pad_rows(q_ref[...], L).astype(BF16)
    k = _pad_rows(k_ref[...], L).astype(BF16)
    v = _pad_rows(v_ref[...], L).astype(BF16)
    g = _pad_rows(g_ref[...], L) + bias_ref[...]
    li = g[:, :LANES]
    lf = _log_sigmoid(g[:, LANES:])
    if lv < L:
        valid = lax.broadcasted_iota(jnp.int32, (L, LANES), 0) < lv
        li = jnp.where(valid, li, -jnp.inf)
        lf = jnp.where(valid, lf, 0.0)

    row = lax.broadcasted_iota(jnp.int32, (L, L), 0)
    col = lax.broadcasted_iota(jnp.int32, (L, L), 1)
    causal = col <= row
    tri = jnp.where(causal, 1.0, 0.0).astype(BF16)
    b = sum(jnp.dot(tri, piece, preferred_element_type=F32) for piece in _split3(lf))
    r = li - b
    r_t = r.T
    m_row = m_scr[...]
    b_last = b[L - 1:L, :]
    m_new = jnp.maximum(b_last + m_row, jnp.max(b_last + r, axis=0, keepdims=True))
    decay = jnp.exp(b_last + m_row - m_new)
    ws_t = jnp.exp(b_last + r - m_new).T
    n_all = n_scr[...]

    for h in range(H_A):
        qh = q[:, h * DK_A:(h + 1) * DK_A]
        kh = k[:, h * DK_A:(h + 1) * DK_A]
        vh = v[:, h * DV_A:(h + 1) * DV_A]
        m0h = m_row[:, h:h + 1]
        rmat = jnp.where(causal, r_t[h:h + 1, :], -jnp.inf)
        mx = jnp.maximum(jnp.max(rmat, axis=1, keepdims=True), m0h)
        w_intra = jnp.exp(rmat - mx)
        w_inter = jnp.exp(m0h - mx) * scale
        qk = lax.dot_general(qh, kh, (((1,), (1,)), ((), ())), preferred_element_type=F32)
        s = qk * scale * w_intra
        ct = ct_scr[h]
        num = (jnp.dot(s.astype(BF16), vh, preferred_element_type=F32)
               + w_inter * jnp.dot(qh, ct.astype(BF16), preferred_element_type=F32))
        qn = jnp.sum(qh.astype(F32) * n_all[h:h + 1, :], axis=1, keepdims=True)
        den = jnp.sum(s, axis=1, keepdims=True) + w_inter * qn
        floor = jnp.exp(-(b[:, h:h + 1] + mx))
        hh = num * (1.0 / jnp.maximum(jnp.abs(den), floor))
        hh = hh * lax.rsqrt(jnp.mean(hh * hh, axis=1, keepdims=True) + EPS)
        hh = hh * hn_ref[:, h * DV_A:(h + 1) * DV_A]
        oh = o_ref[:, h * DV_A:(h + 1) * DV_A].astype(F32)
        zh = z_ref[:, h * DV_A:(h + 1) * DV_A].astype(F32)
        y = hh[:lv] * jax.nn.sigmoid(oh) * _silu(zh)
        act_ref[:, h * DV_A:(h + 1) * DV_A] = y.astype(act_ref.dtype)

        ws_row = ws_t[h:h + 1, :]
        kts = (kh.astype(F32).T * ws_row).astype(BF16)
        dh = decay[:, h:h + 1]
        ct_scr[h] = dh * ct + jnp.dot(kts, vh, preferred_element_type=F32)
        ws8 = jnp.broadcast_to(ws_row, (SUBLANES, L)).astype(BF16)
        n_scr[h:h + 1, :] = dh * n_all[h:h + 1, :] + jnp.dot(ws8, kh, preferred_element_type=F32)[0:1]
    m_scr[...] = m_new

    @pl.when(c == pl.num_programs(1) - 1)
    def _():
        for h in range(H_A):
            c_out[0, h] = ct_scr[h].T
        n_out[0] = n_scr[...]
        m_out[0] = m_scr[...]


def _mlstm(proj, gates, bias, hnorm, c0, n0, m0, bsz, t, act_dtype):
    lv = min(CHUNK, t)
    nc = t // lv
    m = bsz * t
    rows = lambda b, c: b * nc + c
    kern = functools.partial(_mlstm_kernel, lv=lv)
    return pl.pallas_call(
        kern,
        grid=(bsz, nc),
        in_specs=[
            pl.BlockSpec((lv, H_A * DK_A), lambda b, c: (rows(b, c), 0)),
            pl.BlockSpec((lv, H_A * DK_A), lambda b, c: (rows(b, c), 1)),
            pl.BlockSpec((lv, D_INNER), lambda b, c: (rows(b, c), 1)),
            pl.BlockSpec((lv, D_INNER), lambda b, c: (rows(b, c), 2)),
            pl.BlockSpec((lv, D_INNER), lambda b, c: (rows(b, c), 3)),
            pl.BlockSpec((lv, 2 * LANES), lambda b, c: (rows(b, c), 0)),
            pl.BlockSpec((1, 2 * LANES), lambda b, c: (0, 0)),
            pl.BlockSpec((1, D_INNER), lambda b, c: (0, 0)),
            pl.BlockSpec((1, H_A, DV_A, DK_A), lambda b, c: (b, 0, 0, 0)),
            pl.BlockSpec((1, H_A, DK_A), lambda b, c: (b, 0, 0)),
            pl.BlockSpec((1, 1, LANES), lambda b, c: (b, 0, 0)),
        ],
        out_specs=[
            pl.BlockSpec((lv, D_INNER), lambda b, c: (rows(b, c), 0)),
            pl.BlockSpec((1, H_A, DV_A, DK_A), lambda b, c: (b, 0, 0, 0)),
            pl.BlockSpec((1, H_A, DK_A), lambda b, c: (b, 0, 0)),
            pl.BlockSpec((1, 1, LANES), lambda b, c: (b, 0, 0)),
        ],
        out_shape=[
            jax.ShapeDtypeStruct((m, D_INNER), act_dtype),
            jax.ShapeDtypeStruct((bsz, H_A, DV_A, DK_A), F32),
            jax.ShapeDtypeStruct((bsz, H_A, DK_A), F32),
            jax.ShapeDtypeStruct((bsz, 1, LANES), F32),
        ],
        scratch_shapes=[
            pltpu.VMEM((H_A, DK_A, DV_A), F32),
            pltpu.VMEM((H_A, DK_A), F32),
            pltpu.VMEM((1, LANES), F32),
        ],
        compiler_params=_cparams(2),
        name="mlstm_mixer",
    )(proj, proj, proj, proj, proj, gates, bias, hnorm, c0, n0, m0)


def _conv_kernel(bg_ref, cg_ref, xin_ref, z_ref, cw_ref, st_ref, act_ref, st_out, carry):
    tt = bg_ref.shape[0]

    @pl.when(pl.program_id(1) == 0)
    def _():
        carry[...] = st_ref[0]

    u = cg_ref[...].astype(F32) * xin_ref[...].astype(F32)
    p2 = carry[0:1, :]
    p1 = carry[1:2, :]
    row = lax.broadcasted_iota(jnp.int32, u.shape, 0)
    u1 = jnp.where(row == 0, p1, pltpu.roll(u, 1, axis=0))
    u2 = jnp.where(row == 0, p2, jnp.where(row == 1, p1, pltpu.roll(u, 2, axis=0)))
    cw = cw_ref[...]
    conv = cw[0:1, :] * u2 + cw[1:2, :] * u1 + cw[2:3, :] * u
    y = bg_ref[...].astype(F32) * conv * _silu(z_ref[...].astype(F32))
    act_ref[...] = y.astype(act_ref.dtype)
    last = u[tt - (CONV_W - 1):tt, :]
    carry[...] = last
    st_out[0] = last


def _conv(proj, conv_w, state, bsz, t, act_dtype):
    tt = min(256, t)
    nt = t // tt
    m = bsz * t
    blk = lambda j: pl.BlockSpec((tt, D_INNER), lambda b, i: (b * nt + i, j))
    return pl.pallas_call(
        _conv_kernel,
        grid=(bsz, nt),
        in_specs=[
            blk(0), blk(1), blk(2), blk(3),
            pl.BlockSpec((CONV_W, D_INNER), lambda b, i: (0, 0)),
            pl.BlockSpec((1, CONV_W - 1, D_INNER), lambda b, i: (b, 0, 0)),
        ],
        out_specs=[
            pl.BlockSpec((tt, D_INNER), lambda b, i: (b * nt + i, 0)),
            pl.BlockSpec((1, CONV_W - 1, D_INNER), lambda b, i: (b, 0, 0)),
        ],
        out_shape=[
            jax.ShapeDtypeStruct((m, D_INNER), act_dtype),
            jax.ShapeDtypeStruct((bsz, CONV_W - 1, D_INNER), F32),
        ],
        scratch_shapes=[pltpu.VMEM((CONV_W - 1, D_INNER), F32)],
        compiler_params=_cparams(2),
        name="conv_mixer",
    )(proj, proj, proj, proj, conv_w, state)


def _fgate_kernel(g_ref, bias_ref, lf_ref, c_ref, ct_ref, carry, *, lv):
    L = CHUNK

    @pl.when(pl.program_id(1) == 0)
    def _():
        carry[...] = jnp.zeros_like(carry)

    lf = _log_sigmoid(g_ref[...] + bias_ref[...])
    lf_ref[...] = lf
    lf = _pad_rows(lf, L)
    if lv < L:
        lf = jnp.where(lax.broadcasted_iota(jnp.int32, (L, LANES), 0) < lv, lf, 0.0)
    row = lax.broadcasted_iota(jnp.int32, (L, L), 0)
    col = lax.broadcasted_iota(jnp.int32, (L, L), 1)
    tri = jnp.where(col <= row, 1.0, 0.0).astype(BF16)
    cs = sum(jnp.dot(tri, piece, preferred_element_type=F32) for piece in _split3(lf)) + carry[...]
    c_ref[...] = cs[:lv]
    ct_ref[0] = cs.T[:H_C, :]
    carry[...] = cs[L - 1:L, :]


def _fgate(gates, bias, bsz, t):
    lv = min(CHUNK, t)
    nc = t // lv
    m = bsz * t
    kern = functools.partial(_fgate_kernel, lv=lv)
    return pl.pallas_call(
        kern,
        grid=(bsz, nc),
        in_specs=[
            pl.BlockSpec((lv, LANES), lambda b, c: (b * nc + c, 0)),
            pl.BlockSpec((1, LANES), lambda b, c: (0, 0)),
        ],
        out_specs=[
            pl.BlockSpec((lv, LANES), lambda b, c: (b * nc + c, 0)),
            pl.BlockSpec((lv, LANES), lambda b, c: (b * nc + c, 0)),
            pl.BlockSpec((1, H_C, CHUNK), lambda b, c: (b, 0, c)),
        ],
        out_shape=[
            jax.ShapeDtypeStruct((m, LANES), F32),
            jax.ShapeDtypeStruct((m, LANES), F32),
            jax.ShapeDtypeStruct((bsz, H_C, nc * CHUNK), F32),
        ],
        scratch_shapes=[pltpu.VMEM((1, LANES), F32)],
        compiler_params=_cparams(2),
        name="fgate_scan",
    )(gates, bias)


def _flash_kernel(q_ref, k_ref, v_ref, z_ref, cc_ref, cr_ref, o_ref, m_scr, l_scr, acc_scr):
    i = pl.program_id(2)
    j = pl.program_id(3)
    tq = q_ref.shape[0]
    tk = k_ref.shape[0]

    @pl.when(j == 0)
    def _():
        m_scr[...] = jnp.full_like(m_scr, -jnp.inf)
        l_scr[...] = jnp.zeros_like(l_scr)
        acc_scr[...] = jnp.zeros_like(acc_scr)

    @pl.when(j <= i)
    def _():
        q = (q_ref[...] * (DH_C ** -0.5)).astype(BF16)
        k = k_ref[...].astype(BF16)
        s = lax.dot_general(q, k, (((1,), (1,)), ((), ())), preferred_element_type=F32)
        s = s + cc_ref[0, 0] - cr_ref[0, 0]
        qpos = i * tq + lax.broadcasted_iota(jnp.int32, (tq, tk), 0)
        kpos = j * tk + lax.broadcasted_iota(jnp.int32, (tq, tk), 1)
        s = jnp.where(qpos >= kpos, s, -jnp.inf)
        m_prev = m_scr[...]
        m_new = jnp.maximum(m_prev, jnp.max(s, axis=1, keepdims=True))
        corr = jnp.exp(m_prev - m_new)
        p = jnp.exp(s - m_new)
        l_scr[...] = l_scr[...] * corr + jnp.sum(p, axis=1, keepdims=True)
        acc_scr[...] = acc_scr[...] * corr + jnp.dot(
            p.astype(BF16), v_ref[...].astype(BF16), preferred_element_type=F32)
        m_scr[...] = m_new

    @pl.when(j == i)
    def _():
        o = acc_scr[...] * (1.0 / l_scr[...])
        o_ref[...] = (o * _silu(z_ref[...])).astype(o_ref.dtype)


def _flash(proj, c_col, c_row, bsz, t, act_dtype):
    tq = tk = min(512, t)
    nq = t // tq
    m = bsz * t
    return pl.pallas_call(
        _flash_kernel,
        grid=(bsz, H_C, nq, nq),
        in_specs=[
            pl.BlockSpec((tq, DH_C), lambda b, h, i, j: (b * nq + i, h)),
            pl.BlockSpec((tk, DH_C), lambda b, h, i, j: (b * nq + jnp.minimum(i, j), H_C + h)),
            pl.BlockSpec((tk, DH_C), lambda b, h, i, j: (b * nq + jnp.minimum(i, j), 2 * H_C + h)),
            pl.BlockSpec((tq, DH_C), lambda b, h, i, j: (b * nq + i, 3 * H_C + h)),
            pl.BlockSpec((1, 1, tq, 1), lambda b, h, i, j: (b, h, i, 0)),
            pl.BlockSpec((1, 1, 1, tk), lambda b, h, i, j: (b, h, 0, jnp.minimum(i, j))),
        ],
        out_specs=pl.BlockSpec((tq, DH_C), lambda b, h, i, j: (b * nq + i, h)),
        out_shape=jax.ShapeDtypeStruct((m, D_INNER), act_dtype),
        scratch_shapes=[
            pltpu.VMEM((tq, 1), F32),
            pltpu.VMEM((tq, 1), F32),
            pltpu.VMEM((tq, DH_C), F32),
        ],
        compiler_params=_cparams(4),
        name="fox_flash",
    )(proj, proj, proj, proj, c_col, c_row)


def _paged_kernel(pt_ref, q_ref, kn_ref, vn_ref, z_ref, cq_ref, cnt_ref, kp_ref, vp_ref, lfp_ref,
                  out_ref, m_scr, l_scr, acc_scr, carry):
    p = pl.program_id(1)
    t = q_ref.shape[0]
    P = PAGE_SIZE

    @pl.when(p == 0)
    def _():
        m_scr[...] = jnp.full_like(m_scr, -jnp.inf)
        l_scr[...] = jnp.zeros_like(l_scr)
        acc_scr[...] = jnp.zeros_like(acc_scr)
        carry[...] = jnp.zeros_like(carry)

    q = q_ref[...] * (DH_C ** -0.5)
    cq = cq_ref[...]

    def scores(keys, bias_rows):
        rows = []
        for h in range(H_C):
            qh = q[:, h * DH_C:(h + 1) * DH_C].astype(BF16)
            kh = keys[:, h * DH_C:(h + 1) * DH_C].astype(BF16)
            sh = lax.dot_general(qh, kh, (((1,), (1,)), ((), ())), preferred_element_type=F32)
            rows.append(sh + bias_rows[h:h + 1, :] + cq[:, h:h + 1])
        return jnp.concatenate(rows, axis=0)

    def update(s, values):
        m_prev = m_scr[...]
        m_new = jnp.maximum(m_prev, jnp.max(s, axis=1, keepdims=True))
        corr = jnp.exp(m_prev - m_new)
        pr = jnp.exp(s - m_new)
        l_scr[...] = l_scr[...] * corr + jnp.sum(pr, axis=1, keepdims=True)
        pv = []
        for h in range(H_C):
            ph = pr[h * t:(h + 1) * t, :].astype(BF16)
            vh = values[:, h * DH_C:(h + 1) * DH_C].astype(BF16)
            pv.append(jnp.dot(ph, vh, preferred_element_type=F32))
        acc_scr[...] = acc_scr[...] * corr + jnp.concatenate(pv, axis=0)
        m_scr[...] = m_new

    lfp = lfp_ref[0]
    prow = lax.broadcasted_iota(jnp.int32, (P, P), 0)
    pcol = lax.broadcasted_iota(jnp.int32, (P, P), 1)
    later = jnp.where(prow > pcol, 1.0, 0.0).astype(BF16)
    suffix = sum(jnp.dot(piece, later, preferred_element_type=F32) for piece in _split3(lfp)) + carry[...]
    carry[...] = carry[...] + jnp.sum(lfp, axis=1, keepdims=True)
    update(scores(kp_ref[0], suffix), vp_ref[0])

    @pl.when(p == pl.num_programs(1) - 1)
    def _():
        kn = _pad_rows(kn_ref[...], P)
        vn = _pad_rows(vn_ref[...], P)
        s = scores(kn, -cnt_ref[0])
        tok = lax.broadcasted_iota(jnp.int32, (H_C * t, P), 0) % t
        pos = lax.broadcasted_iota(jnp.int32, (H_C * t, P), 1)
        update(jnp.where(pos <= tok, s, -jnp.inf), vn)
        o = acc_scr[...] * (1.0 / l_scr[...])
        for h in range(H_C):
            zh = z_ref[:, h * DH_C:(h + 1) * DH_C]
            out_ref[:, h * DH_C:(h + 1) * DH_C] = (o[h * t:(h + 1) * t, :] * _silu(zh)).astype(out_ref.dtype)


def _paged(page_table, proj, c_q, c_new_t, pool_k, pool_v, pool_lf_t, bsz, t):
    n_pages = page_table.shape[1]
    page = lambda b, p, pt: (pt[b, n_pages - 1 - p], 0, 0)
    blk = lambda j: pl.BlockSpec((t, D_INNER), lambda b, p, pt: (b, j))
    grid_spec = pltpu.PrefetchScalarGridSpec(
        num_scalar_prefetch=1,
        grid=(bsz, n_pages),
        in_specs=[
            blk(0), blk(1), blk(2), blk(3),
            pl.BlockSpec((t, LANES), lambda b, p, pt: (b, 0)),
            pl.BlockSpec((1, H_C, CHUNK), lambda b, p, pt: (b, 0, 0)),
            pl.BlockSpec((1, PAGE_SIZE, D_INNER), page),
            pl.BlockSpec((1, PAGE_SIZE, D_INNER), page),
            pl.BlockSpec((1, H_C, PAGE_SIZE), page),
        ],
        out_specs=pl.BlockSpec((t, D_INNER), lambda b, p, pt: (b, 0)),
        scratch_shapes=[
            pltpu.VMEM((H_C * t, 1), F32),
            pltpu.VMEM((H_C * t, 1), F32),
            pltpu.VMEM((H_C * t, DH_C), F32),
            pltpu.VMEM((H_C, 1), F32),
        ],
    )
    return pl.pallas_call(
        _paged_kernel,
        grid_spec=grid_spec,
        out_shape=jax.ShapeDtypeStruct((bsz * t, D_INNER), F32),
        compiler_params=_cparams(2),
        name="fox_paged",
    )(page_table, proj, proj, proj, proj, c_q, c_new_t, pool_k, pool_v, pool_lf_t)


def _pad_cols(w, n):
    return jnp.pad(w, ((0, 0), (0, n - w.shape[1])))


def _lane_row(vec, offset=0, width=LANES):
    return jnp.zeros((1, width), F32).at[0, offset:offset + vec.shape[0]].set(vec)


def kernel(x_prompt, x_sample, state_mlstm_c, state_mlstm_n, state_mlstm_m, state_conv, cache_k, cache_v, cache_logf, page_table, p_prompt, p_sample, ln_gain, w_in_a, b_igate_a, b_fgate_a, hnorm_a, w_out_a, w_in_b, conv_w_b, w_out_b, w_in_c, b_fgate_c, w_out_c, w_pe, w_pg, pe_norm, final_norm):
    bp, tp, _ = x_prompt.shape
    bs, ts, _ = x_sample.shape
    xp = x_prompt.reshape(bp * tp, D_MODEL)
    xs = x_sample.reshape(bs * ts, D_MODEL)
    final_gain = final_norm.reshape(1, D_MODEL)
    qk = H_A * DK_A
    g0 = 2 * qk + D_INNER
    f0 = 3 * D_INNER

    outs = {name: [] for name in (
        "mc_p", "mn_p", "mm_p", "mc_s", "mn_s", "mm_s", "cv_p", "cv_s",
        "k_p", "v_p", "f_p", "k_s", "v_s", "f_s")}
    yp = ys = None
    for i in range(DEPTH):
        j = i // N_MIXERS
        gain = ln_gain[i].reshape(1, D_MODEL)
        kind = i % N_MIXERS
        if kind == 0:
            w = w_in_a[j]
            w_main = jnp.concatenate([w[:, :g0], w[:, g0 + 2 * H_A:]], axis=1).astype(BF16)
            w_gate = jnp.concatenate([_pad_cols(w[:, g0:g0 + H_A], LANES),
                                      _pad_cols(w[:, g0 + H_A:g0 + 2 * H_A], LANES)], axis=1).astype(BF16)
            bias = jnp.concatenate([_lane_row(b_igate_a[j]), _lane_row(b_fgate_a[j])], axis=1)
            hn = hnorm_a[j].reshape(1, D_INNER)
            w_out = w_out_a[j]

            proj_p, gates_p = _proj(xp, gain, w_main, w_gate, BF16)
            act_p, c_p, n_p, m_p = _mlstm(
                proj_p, gates_p, bias, hn,
                jnp.zeros((bp, H_A, DV_A, DK_A), F32), jnp.zeros((bp, H_A, DK_A), F32),
                jnp.zeros((bp, 1, LANES), F32), bp, tp, BF16)
            proj_s, gates_s = _proj(xs, gain, w_main, w_gate, F32)
            m0 = jnp.pad(state_mlstm_m[j], ((0, 0), (0, LANES - H_A))).reshape(bs, 1, LANES)
            act_s, c_s, n_s, m_s = _mlstm(
                proj_s, gates_s, bias, hn, state_mlstm_c[j], state_mlstm_n[j], m0, bs, ts, F32)
            outs["mc_p"].append(c_p); outs["mn_p"].append(n_p); outs["mm_p"].append(m_p[:, 0, :H_A])
            outs["mc_s"].append(c_s); outs["mn_s"].append(n_s); outs["mm_s"].append(m_s[:, 0, :H_A])
        elif kind == 1:
            w_main = w_in_b[j].astype(BF16)
            w_gate = jnp.zeros((D_MODEL, LANES), BF16)
            w_out = w_out_b[j]
            proj_p, _ = _proj(xp, gain, w_main, w_gate, BF16)
            act_p, st_p = _conv(proj_p, conv_w_b[j], jnp.zeros((bp, CONV_W - 1, D_INNER), F32), bp, tp, BF16)
            proj_s, _ = _proj(xs, gain, w_main, w_gate, F32)
            act_s, st_s = _conv(proj_s, conv_w_b[j], state_conv[j], bs, ts, F32)
            outs["cv_p"].append(st_p); outs["cv_s"].append(st_s)
        else:
            w = w_in_c[j]
            w_main = jnp.concatenate([w[:, :f0], w[:, f0 + H_C:]], axis=1).astype(BF16)
            w_gate = _pad_cols(w[:, f0:f0 + H_C], LANES).astype(BF16)
            bias = _lane_row(b_fgate_c[j])
            w_out = w_out_c[j]

            proj_p, gates_p = _proj(xp, gain, w_main, w_gate, F32)
            lf_p, c_p, ct_p = _fgate(gates_p, bias, bp, tp)
            c_col = c_p.reshape(bp, tp, LANES)[:, :, :H_C].transpose(0, 2, 1)[..., None]
            c_row = ct_p[:, :, None, :]
            act_p = _flash(proj_p, c_col, c_row, bp, tp, BF16)
            outs["k_p"].append(proj_p[:, D_INNER:2 * D_INNER].reshape(bp, tp, H_C, DH_C))
            outs["v_p"].append(proj_p[:, 2 * D_INNER:3 * D_INNER].reshape(bp, tp, H_C, DH_C))
            outs["f_p"].append(lf_p[:, :H_C].reshape(bp, tp, H_C))

            proj_s, gates_s = _proj(xs, gain, w_main, w_gate, F32)
            lf_s, c_s, ct_s = _fgate(gates_s, bias, bs, ts)
            n_pool = cache_k.shape[1]
            pool_k = cache_k[j].reshape(n_pool, PAGE_SIZE, D_INNER)
            pool_v = cache_v[j].reshape(n_pool, PAGE_SIZE, D_INNER)
            pool_lf_t = cache_logf[j].transpose(0, 2, 1)
            act_s = _paged(page_table, proj_s, c_s, ct_s, pool_k, pool_v, pool_lf_t, bs, ts)
            outs["k_s"].append(proj_s[:, D_INNER:2 * D_INNER].reshape(bs, ts, H_C, DH_C))
            outs["v_s"].append(proj_s[:, 2 * D_INNER:3 * D_INNER].reshape(bs, ts, H_C, DH_C))
            outs["f_s"].append(lf_s[:, :H_C].reshape(bs, ts, H_C))

        final = i == DEPTH - 1
        w_out = w_out.astype(BF16)
        wpe = w_pe[i].astype(BF16)
        wpg = w_pg[i].astype(BF16)
        pg = pe_norm[i].reshape(1, D_MODEL)
        xp, yp = _out_embed(act_p, xp, p_prompt[i].reshape(bp * tp, PE_DIM), w_out, wpe, wpg, pg, final_gain, final)
        xs, ys = _out_embed(act_s, xs, p_sample[i].reshape(bs * ts, PE_DIM), w_out, wpe, wpg, pg, final_gain, final)

    st = lambda name: jnp.stack(outs[name])
    return (yp.reshape(bp, tp, D_MODEL), ys.reshape(bs, ts, D_MODEL),
            st("mc_p"), st("mn_p"), st("mm_p"), st("mc_s"), st("mn_s"), st("mm_s"),
            st("cv_p"), st("cv_s"),
            st("k_p"), st("v_p"), st("f_p"), st("k_s"), st("v_s"), st("f_s"))
```

```python
import functools

import jax
import jax.numpy as jnp
from jax import lax
from jax.experimental import pallas as pl
from jax.experimental.pallas import tpu as pltpu

F32 = jnp.float32
BF16 = jnp.bfloat16

EPS = 1e-6
DEPTH = 4
N_MIXERS = 3
D_MODEL = 1024
D_INNER = 2 * D_MODEL
H_A = 8
DK_A = D_INNER // (2 * H_A)
DV_A = D_INNER // H_A
CONV_W = 3
DH_C = 128
H_C = D_INNER // DH_C
PE_DIM = 256
PAGE_SIZE = 128
LOG2E = 1.4426950408889634

LANES = 128
SUBLANES = 8
CHUNK = 128
VMEM_LIMIT_BYTES = 48 * 1024 * 1024


def _cparams(n_axes):
    return pltpu.CompilerParams(
        dimension_semantics=("arbitrary",) * n_axes,
        vmem_limit_bytes=VMEM_LIMIT_BYTES)


def _log_sigmoid(x):
    return jnp.minimum(x, 0.0) - jnp.log1p(jnp.exp(-jnp.abs(x)))


def _silu(x):
    return x * jax.nn.sigmoid(x)


def _rms(x, gain):
    return x * lax.rsqrt(jnp.mean(x * x, axis=-1, keepdims=True) + EPS) * gain


def _split3(x):
    hi = x.astype(BF16)
    r1 = x - hi.astype(F32)
    mid = r1.astype(BF16)
    lo = (r1 - mid.astype(F32)).astype(BF16)
    return hi, mid, lo


def _pad_rows(x, rows):
    if x.shape[0] == rows:
        return x
    return jnp.concatenate([x, jnp.zeros((rows - x.shape[0], x.shape[1]), x.dtype)], axis=0)


PROJ_TN = 1024


def _proj_kernel(x_ref, g_ref, w_ref, wg_ref, cs_ref, o_ref, og_ref, *rest, f32_tiles, scaled):
    extra_refs, h_scr = rest[:-1], rest[-1]
    j = pl.program_id(1)

    @pl.when(j == 0)
    def _():
        h = _rms(x_ref[...], g_ref[...]).astype(BF16)
        h_scr[...] = h
        og_ref[...] = jnp.dot(h, wg_ref[...], preferred_element_type=F32)

    res = jnp.dot(h_scr[...], w_ref[j], preferred_element_type=F32)
    if scaled:
        res = res * cs_ref[j]
    o_ref[...] = res.astype(o_ref.dtype)
    for ref, (lo, hi) in zip(extra_refs, f32_tiles):
        @pl.when((j >= lo) & (j < hi))
        def _(ref=ref):
            ref[...] = res


def _proj(x, gain, w_tiles, wg, out_dtype, col_scale=None, f32_tiles=()):
    m, d = x.shape
    nt, _, tn = w_tiles.shape
    ng = wg.shape[1]
    tm = min(512, m)
    scaled = col_scale is not None
    cs = (col_scale if scaled else jnp.ones((nt * tn,), F32)).reshape(nt, 1, tn)
    resident = lambda shape: pl.BlockSpec(shape, lambda i, j: (0,) * len(shape), pipeline_mode=pl.Buffered(1))
    extra_specs = [pl.BlockSpec((tm, tn), lambda i, j, lo=lo, hi=hi: (i, jnp.clip(j - lo, 0, hi - lo - 1)))
                   for lo, hi in f32_tiles]
    extra_shapes = [jax.ShapeDtypeStruct((m, (hi - lo) * tn), F32) for lo, hi in f32_tiles]
    kern = functools.partial(_proj_kernel, f32_tiles=tuple(f32_tiles), scaled=scaled)
    return pl.pallas_call(
        kern,
        grid=(m // tm, nt),
        in_specs=[
            pl.BlockSpec((tm, d), lambda i, j: (i, 0)),
            pl.BlockSpec((1, d), lambda i, j: (0, 0)),
            resident((nt, d, tn)),
            resident((d, ng)),
            resident((nt, 1, tn)),
        ],
        out_specs=[
            pl.BlockSpec((tm, tn), lambda i, j: (i, j)),
            pl.BlockSpec((tm, ng), lambda i, j: (i, 0)),
        ] + extra_specs,
        out_shape=[
            jax.ShapeDtypeStruct((m, nt * tn), out_dtype),
            jax.ShapeDtypeStruct((m, ng), F32),
        ] + extra_shapes,
        scratch_shapes=[pltpu.VMEM((tm, d), BF16)],
        compiler_params=_cparams(2),
        name="norm_proj",
    )(x, gain, w_tiles, wg, cs)


def _tile_cols(w):
    d, n = w.shape
    return w.reshape(d, n // PROJ_TN, PROJ_TN).transpose(1, 0, 2).astype(BF16)


def _out_embed_kernel(act_ref, x_ref, p_ref, wo_ref, wpe_ref, wpg_ref, g_ref, fg_ref,
                      xo_ref, *maybe_y_ref):
    y = jnp.dot(act_ref[...].astype(BF16), wo_ref[...], preferred_element_type=F32)
    x1 = x_ref[...] + y
    rn = _rms(x1, g_ref[...]).astype(BF16)
    gate = jax.nn.sigmoid(jnp.dot(rn, wpg_ref[...], preferred_element_type=F32))
    pe = jnp.dot(p_ref[...].astype(BF16), wpe_ref[...], preferred_element_type=F32)
    x2 = x1 + pe * gate
    xo_ref[...] = x2
    if maybe_y_ref:
        maybe_y_ref[0][...] = _rms(x2, fg_ref[...])


def _out_embed(act, x, p, wo, wpe, wpg, gain, final_gain, final):
    m, d = x.shape
    di = act.shape[1]
    pe = p.shape[1]
    tm = min(512, m)
    row = lambda i: (i, 0)
    const = lambda i: (0, 0)
    n_out = 2 if final else 1
    outs = pl.pallas_call(
        _out_embed_kernel,
        grid=(m // tm,),
        in_specs=[
            pl.BlockSpec((tm, di), row),
            pl.BlockSpec((tm, d), row),
            pl.BlockSpec((tm, pe), row),
            pl.BlockSpec((di, d), const),
            pl.BlockSpec((pe, d), const),
            pl.BlockSpec((d, d), const),
            pl.BlockSpec((1, d), const),
            pl.BlockSpec((1, d), const),
        ],
        out_specs=[pl.BlockSpec((tm, d), row)] * n_out,
        out_shape=[jax.ShapeDtypeStruct((m, d), F32)] * n_out,
        compiler_params=_cparams(1),
        name="out_embed",
    )(act, x, p, wo, wpe, wpg, gain, final_gain)
    return outs if final else (outs[0], None)


def _mlstm_kernel(q_ref, k_ref, v_ref, o_ref, z_ref, g_ref, bias_ref, hn_ref,
                  c0_ref, n0_ref, m0_ref,
                  act_ref, c_out, n_out, m_out,
                  ct_scr, n_scr, m_scr, *, lv):
    c = pl.program_id(1)
    L = CHUNK
    scale = DK_A ** -0.5

    @pl.when(c == 0)
    def _():
        for h in range(H_A):
            ct_scr[h] = c0_ref[0, h].T
        n_scr[...] = n0_ref[0]
        m_scr[...] = m0_ref[0]

    q = _pad_rows(q_ref[...], L).astype(BF16)
    k = _pad_rows(k_ref[...], L).astype(BF16)
    v = _pad_rows(v_ref[...], L).astype(BF16)
    g = _pad_rows(g_ref[...], L) + bias_ref[...]
    li = g[:, :LANES]
    lf = _log_sigmoid(g[:, LANES:])
    if lv < L:
        valid = lax.broadcasted_iota(jnp.int32, (L, LANES), 0) < lv
        li = jnp.where(valid, li, -jnp.inf)
        lf = jnp.where(valid, lf, 0.0)

    row = lax.broadcasted_iota(jnp.int32, (L, L), 0)
    col = lax.broadcasted_iota(jnp.int32, (L, L), 1)
    causal = col <= row
    tri = jnp.where(causal, 1.0, 0.0).astype(BF16)
    b = sum(jnp.dot(tri, piece, preferred_element_type=F32) for piece in _split3(lf))
    r = li - b
    r_t = r.T
    m_row = m_scr[...]
    b_last = b[L - 1:L, :]
    m_new = jnp.maximum(b_last + m_row, jnp.max(b_last + r, axis=0, keepdims=True))
    decay = jnp.exp(b_last + m_row - m_new)
    ws_t = jnp.exp(b_last + r - m_new).T
    n_all = n_scr[...]

    for h in range(H_A):
        qh = q[:, h * DK_A:(h + 1) * DK_A]
        kh = k[:, h * DK_A:(h + 1) * DK_A]
        vh = v[:, h * DV_A:(h + 1) * DV_A]
        m0h = m_row[:, h:h + 1]
        rmat = jnp.where(causal, r_t[h:h + 1, :], -jnp.inf)
        mx = jnp.maximum(jnp.max(rmat, axis=1, keepdims=True), m0h)
        w_intra = jnp.exp(rmat - mx)
        w_inter = jnp.exp(m0h - mx) * scale
        qk = lax.dot_general(qh, kh, (((1,), (1,)), ((), ())), preferred_element_type=F32)
        s = qk * scale * w_intra
        ct = ct_scr[h]
        num = (jnp.dot(s.astype(BF16), vh, preferred_element_type=F32)
               + w_inter * jnp.dot(qh, ct.astype(BF16), preferred_element_type=F32))
        qn = jnp.sum(qh.astype(F32) * n_all[h:h + 1, :], axis=1, keepdims=True)
        den = jnp.sum(s, axis=1, keepdims=True) + w_inter * qn
        floor = jnp.exp(-(b[:, h:h + 1] + mx))
        hh = num * (1.0 / jnp.maximum(jnp.abs(den), floor))
        hh = hh * lax.rsqrt(jnp.mean(hh * hh, axis=1, keepdims=True) + EPS)
        hh = hh * hn_ref[:, h * DV_A:(h + 1) * DV_A]
        oh = o_ref[:, h * DV_A:(h + 1) * DV_A].astype(F32)
        zh = z_ref[:, h * DV_A:(h + 1) * DV_A].astype(F32)
        y = hh[:lv] * jax.nn.sigmoid(oh) * _silu(zh)
        act_ref[:, h * DV_A:(h + 1) * DV_A] = y.astype(act_ref.dtype)

        ws_row = ws_t[h:h + 1, :]
        kts = (kh.astype(F32).T * ws_row).astype(BF16)
        dh = decay[:, h:h + 1]
        ct_scr[h] = dh * ct + jnp.dot(kts, vh, preferred_element_type=F32)
        ws8 = jnp.broadcast_to(ws_row, (SUBLANES, L)).astype(BF16)
        n_scr[h:h + 1, :] = dh * n_all[h:h + 1, :] + jnp.dot(ws8, kh, preferred_element_type=F32)[0:1]
    m_scr[...] = m_new

    @pl.when(c == pl.num_programs(1) - 1)
    def _():
        for h in range(H_A):
            c_out[0, h] = ct_scr[h].T
        n_out[0] = n_scr[...]
        m_out[0] = m_scr[...]


def _mlstm(proj, gates, bias, hnorm, c0, n0, m0, bsz, t, act_dtype):
    lv = min(CHUNK, t)
    nc = t // lv
    m = bsz * t
    rows = lambda b, c: b * nc + c
    kern = functools.partial(_mlstm_kernel, lv=lv)
    return pl.pallas_call(
        kern,
        grid=(bsz, nc),
        in_specs=[
            pl.BlockSpec((lv, H_A * DK_A), lambda b, c: (rows(b, c), 0)),
            pl.BlockSpec((lv, H_A * DK_A), lambda b, c: (rows(b, c), 1)),
            pl.BlockSpec((lv, D_INNER), lambda b, c: (rows(b, c), 1)),
            pl.BlockSpec((lv, D_INNER), lambda b, c: (rows(b, c), 2)),
            pl.BlockSpec((lv, D_INNER), lambda b, c: (rows(b, c), 3)),
            pl.BlockSpec((lv, 2 * LANES), lambda b, c: (rows(b, c), 0)),
            pl.BlockSpec((1, 2 * LANES), lambda b, c: (0, 0)),
            pl.BlockSpec((1, D_INNER), lambda b, c: (0, 0)),
            pl.BlockSpec((1, H_A, DV_A, DK_A), lambda b, c: (b, 0, 0, 0)),
            pl.BlockSpec((1, H_A, DK_A), lambda b, c: (b, 0, 0)),
            pl.BlockSpec((1, 1, LANES), lambda b, c: (b, 0, 0)),
        ],
        out_specs=[
            pl.BlockSpec((lv, D_INNER), lambda b, c: (rows(b, c), 0)),
            pl.BlockSpec((1, H_A, DV_A, DK_A), lambda b, c: (b, 0, 0, 0)),
            pl.BlockSpec((1, H_A, DK_A), lambda b, c: (b, 0, 0)),
            pl.BlockSpec((1, 1, LANES), lambda b, c: (b, 0, 0)),
        ],
        out_shape=[
            jax.ShapeDtypeStruct((m, D_INNER), act_dtype),
            jax.ShapeDtypeStruct((bsz, H_A, DV_A, DK_A), F32),
            jax.ShapeDtypeStruct((bsz, H_A, DK_A), F32),
            jax.ShapeDtypeStruct((bsz, 1, LANES), F32),
        ],
        scratch_shapes=[
            pltpu.VMEM((H_A, DK_A, DV_A), F32),
            pltpu.VMEM((H_A, DK_A), F32),
            pltpu.VMEM((1, LANES), F32),
        ],
        compiler_params=_cparams(2),
        name="mlstm_mixer",
    )(proj, proj, proj, proj, proj, gates, bias, hnorm, c0, n0, m0)


def _conv_kernel(bg_ref, cg_ref, xin_ref, z_ref, cw_ref, st_ref, act_ref, st_out, carry):
    tt = bg_ref.shape[0]

    @pl.when(pl.program_id(1) == 0)
    def _():
        carry[...] = st_ref[0]

    u = cg_ref[...].astype(F32) * xin_ref[...].astype(F32)
    p2 = carry[0:1, :]
    p1 = carry[1:2, :]
    row = lax.broadcasted_iota(jnp.int32, u.shape, 0)
    u1 = jnp.where(row == 0, p1, pltpu.roll(u, 1, axis=0))
    u2 = jnp.where(row == 0, p2, jnp.where(row == 1, p1, pltpu.roll(u, 2, axis=0)))
    cw = cw_ref[...]
    conv = cw[0:1, :] * u2 + cw[1:2, :] * u1 + cw[2:3, :] * u
    y = bg_ref[...].astype(F32) * conv * _silu(z_ref[...].astype(F32))
    act_ref[...] = y.astype(act_ref.dtype)
    last = u[tt - (CONV_W - 1):tt, :]
    carry[...] = last
    st_out[0] = last


def _conv(proj, conv_w, state, bsz, t, act_dtype):
    tt = min(256, t)
    nt = t // tt
    m = bsz * t
    blk = lambda j: pl.BlockSpec((tt, D_INNER), lambda b, i: (b * nt + i, j))
    return pl.pallas_call(
        _conv_kernel,
        grid=(bsz, nt),
        in_specs=[
            blk(0), blk(1), blk(2), blk(3),
            pl.BlockSpec((CONV_W, D_INNER), lambda b, i: (0, 0)),
            pl.BlockSpec((1, CONV_W - 1, D_INNER), lambda b, i: (b, 0, 0)),
        ],
        out_specs=[
            pl.BlockSpec((tt, D_INNER), lambda b, i: (b * nt + i, 0)),
            pl.BlockSpec((1, CONV_W - 1, D_INNER), lambda b, i: (b, 0, 0)),
        ],
        out_shape=[
            jax.ShapeDtypeStruct((m, D_INNER), act_dtype),
            jax.ShapeDtypeStruct((bsz, CONV_W - 1, D_INNER), F32),
        ],
        scratch_shapes=[pltpu.VMEM((CONV_W - 1, D_INNER), F32)],
        compiler_params=_cparams(2),
        name="conv_mixer",
    )(proj, proj, proj, proj, conv_w, state)


def _fgate_kernel(g_ref, bias_ref, lf_ref, c_ref, *rest, lv, with_aux):
    carry = rest[-1]
    L = CHUNK

    @pl.when(pl.program_id(1) == 0)
    def _():
        carry[...] = jnp.zeros_like(carry)

    lf = _log_sigmoid(g_ref[...] + bias_ref[...])
    lf_ref[...] = lf
    lf = _pad_rows(lf, L)
    if lv < L:
        lf = jnp.where(lax.broadcasted_iota(jnp.int32, (L, LANES), 0) < lv, lf, 0.0)
    row = lax.broadcasted_iota(jnp.int32, (L, L), 0)
    col = lax.broadcasted_iota(jnp.int32, (L, L), 1)
    tri = jnp.where(col <= row, 1.0, 0.0).astype(BF16)
    cs = sum(jnp.dot(tri, piece, preferred_element_type=F32) for piece in _split3(lf)) + carry[...]
    c_ref[...] = cs[:lv]
    carry[...] = cs[L - 1:L, :]
    if with_aux:
        qa_ref, ka_ref = rest[0], rest[1]
        lane = lax.broadcasted_iota(jnp.int32, (L, LANES), 1)
        for h in range(H_C):
            hi, mid, lo = (piece.astype(F32) for piece in _split3(cs[:, h:h + 1] * LOG2E))
            qa = jnp.where(lane == 0, hi, jnp.where(lane == 1, mid, jnp.where(
                lane == 2, lo, jnp.where(lane < 6, 1.0, 0.0))))
            ka = jnp.where(lane < 3, 1.0, jnp.where(lane == 3, -hi, jnp.where(
                lane == 4, -mid, jnp.where(lane == 5, -lo, 0.0))))
            qa_ref[0, h] = qa.astype(BF16)
            ka_ref[0, h] = ka.astype(BF16)


def _fgate(gates, bias, bsz, t, with_aux):
    lv = min(CHUNK, t)
    nc = t // lv
    m = bsz * t
    rows = pl.BlockSpec((lv, LANES), lambda b, c: (b * nc + c, 0))
    aux_spec = pl.BlockSpec((1, H_C, lv, LANES), lambda b, c: (b, 0, c, 0))
    aux_shape = jax.ShapeDtypeStruct((bsz, H_C, t, LANES), BF16)
    kern = functools.partial(_fgate_kernel, lv=lv, with_aux=with_aux)
    return pl.pallas_call(
        kern,
        grid=(bsz, nc),
        in_specs=[rows, pl.BlockSpec((1, LANES), lambda b, c: (0, 0))],
        out_specs=[rows, rows] + [aux_spec] * (2 * with_aux),
        out_shape=[jax.ShapeDtypeStruct((m, LANES), F32)] * 2 + [aux_shape] * (2 * with_aux),
        scratch_shapes=[pltpu.VMEM((1, LANES), F32)],
        compiler_params=_cparams(2),
        name="fgate_scan",
    )(gates, bias)


def _online_softmax_step(s_blocks, m, l, acc, values):
    bmax = functools.reduce(jnp.maximum, s_blocks)
    m_new = jnp.maximum(m, jnp.max(bmax, axis=1, keepdims=True))
    corr = jnp.exp2(m - m_new)
    ps = [jnp.exp2(blk - m_new) for blk in s_blocks]
    l_new = l * corr + functools.reduce(jnp.add, ps)
    p = ps[0] if len(ps) == 1 else jnp.concatenate(ps, axis=1)
    acc_new = acc * corr + jnp.dot(p.astype(BF16), values, preferred_element_type=F32)
    return m_new, l_new, acc_new


def _lane_blocks(s):
    return [s[:, b * LANES:(b + 1) * LANES] for b in range(s.shape[1] // LANES)]


FLASH_T = 512


def _flash_kernel(q_ref, k_ref, v_ref, z_ref, qa_ref, ka_ref, o_ref, *, tq):
    nq = q_ref.shape[0] // tq
    causal = (lax.broadcasted_iota(jnp.int32, (tq, tq), 1)
              <= lax.broadcasted_iota(jnp.int32, (tq, tq), 0))
    for i in range(nq):
        rs = slice(i * tq, (i + 1) * tq)
        q_aug = jnp.concatenate([q_ref[rs, :], qa_ref[0, 0, rs, :]], axis=1)
        m = jnp.full((tq, LANES), -jnp.inf, F32)
        l = jnp.zeros((tq, LANES), F32)
        acc = jnp.zeros((tq, DH_C), F32)
        for j in range(i + 1):
            ks = slice(j * tq, (j + 1) * tq)
            k_aug = jnp.concatenate([k_ref[ks, :], ka_ref[0, 0, ks, :]], axis=1)
            s = lax.dot_general(q_aug, k_aug, (((1,), (1,)), ((), ())), preferred_element_type=F32)
            if j == i:
                s = jnp.where(causal, s, -jnp.inf)
            m, l, acc = _online_softmax_step(_lane_blocks(s), m, l, acc, v_ref[ks, :])
        o = acc * (1.0 / jnp.sum(l, axis=1, keepdims=True))
        o_ref[rs, :] = (o * _silu(z_ref[rs, :].astype(F32))).astype(o_ref.dtype)


def _flash(proj, q_aux, k_aux, bsz, t):
    tq = min(FLASH_T, t)
    col = lambda j: pl.BlockSpec((t, DH_C), lambda b, h: (b, j * H_C + h))
    aux = pl.BlockSpec((1, 1, t, LANES), lambda b, h: (b, h, 0, 0))
    return pl.pallas_call(
        functools.partial(_flash_kernel, tq=tq),
        grid=(bsz, H_C),
        in_specs=[col(0), col(1), col(2), col(3), aux, aux],
        out_specs=pl.BlockSpec((t, DH_C), lambda b, h: (b, h)),
        out_shape=jax.ShapeDtypeStruct((bsz * t, D_INNER), BF16),
        compiler_params=_cparams(2),
        name="fox_flash",
    )(proj, proj, proj, proj, q_aux, k_aux)


PAGES_PER_STEP = 4
PAGED_COLS = PAGE_SIZE * H_C


def _paged_kernel(pt_ref, q_ref, kn_ref, vn_ref, z_ref, cqc_ref, cnf_ref, *rest, t):
    page_refs = rest[:3 * PAGES_PER_STEP]
    out_ref, qs_scr, mask_scr, spread_scr, m_scr, l_scr, acc_scr, carry = rest[3 * PAGES_PER_STEP:]
    p = pl.program_id(1)
    P = PAGE_SIZE
    R = H_C * t
    W = P * H_C
    t_shift = t.bit_length() - 1
    h_shift = H_C.bit_length() - 1

    @pl.when(p == 0)
    def _():
        m_scr[...] = jnp.full_like(m_scr, -jnp.inf)
        l_scr[...] = jnp.zeros_like(l_scr)
        acc_scr[...] = jnp.zeros_like(acc_scr)
        carry[...] = jnp.zeros_like(carry)
        q = q_ref[...] * (DH_C ** -0.5 * LOG2E)
        qs_scr[...] = jnp.concatenate(
            [q[:, h * DH_C:(h + 1) * DH_C] for h in range(H_C)], axis=0).astype(BF16)
        row_h = lax.shift_right_logical(lax.broadcasted_iota(jnp.int32, (R, W), 0), t_shift)
        col_h = lax.broadcasted_iota(jnp.int32, (R, W), 1) & (H_C - 1)
        mask_scr[...] = jnp.where(row_h == col_h, cqc_ref[0] * LOG2E, -jnp.inf)
        pos = lax.broadcasted_iota(jnp.int32, (P, W), 0)
        col_p = lax.shift_right_logical(lax.broadcasted_iota(jnp.int32, (P, W), 1), h_shift)
        spread_scr[...] = jnp.where(pos > col_p, 1.0, 0.0).astype(BF16)

    qs = qs_scr[...]
    head = lax.broadcasted_iota(jnp.int32, (H_C, W), 0)
    col_h16 = lax.broadcasted_iota(jnp.int32, (H_C, W), 1) & (H_C - 1)
    state = (m_scr[...], l_scr[...], acc_scr[...])
    suffix_carry = carry[...]
    for kp_ref, vp_ref, lfp_ref in zip(*[page_refs[i::3] for i in range(3)]):
        lfp = lfp_ref[0]
        g = sum(jnp.dot(piece, spread_scr[...], preferred_element_type=F32)
                for piece in _split3(lfp)) + suffix_carry
        bias = jnp.sum(jnp.where(head == col_h16, g, 0.0), axis=0, keepdims=True) * LOG2E
        suffix_carry = suffix_carry + jnp.sum(lfp, axis=1, keepdims=True)
        for c in range(W // PAGED_COLS):
            cols = slice(c * PAGED_COLS, (c + 1) * PAGED_COLS)
            s = lax.dot_general(qs, kp_ref[0, cols, :].astype(BF16), (((1,), (1,)), ((), ())),
                                preferred_element_type=F32)
            state = _online_softmax_step(_lane_blocks(s + mask_scr[:, cols] + bias[:, cols]), *state,
                                         vp_ref[0, cols, :].astype(BF16))
    carry[...] = suffix_carry
    m_scr[...], l_scr[...], acc_scr[...] = state

    @pl.when(p == pl.num_programs(1) - 1)
    def _():
        sn = lax.dot_general(qs, kn_ref[0].astype(BF16), (((1,), (1,)), ((), ())), preferred_element_type=F32)
        row = lax.broadcasted_iota(jnp.int32, (R, R), 0)
        col = lax.broadcasted_iota(jnp.int32, (R, R), 1)
        valid = (((col & (H_C - 1)) == lax.shift_right_logical(row, t_shift))
                 & (lax.shift_right_logical(col, h_shift) <= (row & (t - 1))))
        sn = jnp.where(valid, sn + (cqc_ref[0] - cnf_ref[0]) * LOG2E, -jnp.inf)
        _, l, acc = _online_softmax_step([sn], m_scr[...], l_scr[...], acc_scr[...], vn_ref[0].astype(BF16))
        o = acc * (1.0 / jnp.sum(l, axis=1, keepdims=True))
        for h in range(H_C):
            zh = z_ref[:, h * DH_C:(h + 1) * DH_C]
            out_ref[:, h * DH_C:(h + 1) * DH_C] = (o[h * t:(h + 1) * t, :] * _silu(zh)).astype(out_ref.dtype)


def _paged(page_table, proj, kn_rows, vn_rows, cq_col, cn_flat, pool_k, pool_v, pool_lf_t, bsz, t):
    assert t & (t - 1) == 0 and t * H_C == LANES, "sample length must be a power of two with t*H == 128"
    n_pages = page_table.shape[1]
    g = PAGES_PER_STEP
    assert n_pages % g == 0
    R = H_C * t
    W = PAGE_SIZE * H_C
    per_b = lambda b, p, pt: (b, 0, 0)
    page_specs = []
    for i in range(g):
        page = lambda b, p, pt, i=i: (pt[b, n_pages - 1 - (p * g + i)], 0, 0)
        page_specs += [pl.BlockSpec((1, W, DH_C), page), pl.BlockSpec((1, W, DH_C), page),
                       pl.BlockSpec((1, H_C, PAGE_SIZE), page)]
    grid_spec = pltpu.PrefetchScalarGridSpec(
        num_scalar_prefetch=1,
        grid=(bsz, n_pages // g),
        in_specs=[
            pl.BlockSpec((t, D_INNER), lambda b, p, pt: (b, 0)),
            pl.BlockSpec((1, R, DH_C), per_b),
            pl.BlockSpec((1, R, DH_C), per_b),
            pl.BlockSpec((t, D_INNER), lambda b, p, pt: (b, 3)),
            pl.BlockSpec((1, R, 1), per_b),
            pl.BlockSpec((1, 1, R), per_b),
        ] + page_specs,
        out_specs=pl.BlockSpec((t, D_INNER), lambda b, p, pt: (b, 0)),
        scratch_shapes=[
            pltpu.VMEM((R, DH_C), BF16),
            pltpu.VMEM((R, W), F32),
            pltpu.VMEM((PAGE_SIZE, W), BF16),
            pltpu.VMEM((R, LANES), F32),
            pltpu.VMEM((R, LANES), F32),
            pltpu.VMEM((R, DH_C), F32),
            pltpu.VMEM((H_C, 1), F32),
        ],
    )
    return pl.pallas_call(
        functools.partial(_paged_kernel, t=t),
        grid_spec=grid_spec,
        out_shape=jax.ShapeDtypeStruct((bsz * t, D_INNER), F32),
        compiler_params=_cparams(2),
        name="fox_paged",
    )(page_table, proj, kn_rows, vn_rows, proj, cq_col, cn_flat, *([pool_k, pool_v, pool_lf_t] * g))


def _pad_cols(w, n):
    return jnp.pad(w, ((0, 0), (0, n - w.shape[1])))


def _lane_row(vec, offset=0, width=LANES):
    return jnp.zeros((1, width), F32).at[0, offset:offset + vec.shape[0]].set(vec)


def kernel(x_prompt, x_sample, state_mlstm_c, state_mlstm_n, state_mlstm_m, state_conv, cache_k, cache_v, cache_logf, page_table, p_prompt, p_sample, ln_gain, w_in_a, b_igate_a, b_fgate_a, hnorm_a, w_out_a, w_in_b, conv_w_b, w_out_b, w_in_c, b_fgate_c, w_out_c, w_pe, w_pg, pe_norm, final_norm):
    bp, tp, _ = x_prompt.shape
    bs, ts, _ = x_sample.shape
    xp = x_prompt.reshape(bp * tp, D_MODEL)
    xs = x_sample.reshape(bs * ts, D_MODEL)
    final_gain = final_norm.reshape(1, D_MODEL)
    qk = H_A * DK_A
    g0 = 2 * qk + D_INNER
    f0 = 3 * D_INNER

    outs = {name: [] for name in (
        "mc_p", "mn_p", "mm_p", "mc_s", "mn_s", "mm_s", "cv_p", "cv_s",
        "k_p", "v_p", "f_p", "k_s", "v_s", "f_s")}
    yp = ys = None
    for i in range(DEPTH):
        j = i // N_MIXERS
        gain = ln_gain[i].reshape(1, D_MODEL)
        kind = i % N_MIXERS
        if kind == 0:
            w = w_in_a[j]
            w_main = _tile_cols(jnp.concatenate([w[:, :g0], w[:, g0 + 2 * H_A:]], axis=1))
            w_gate = jnp.concatenate([_pad_cols(w[:, g0:g0 + H_A], LANES),
                                      _pad_cols(w[:, g0 + H_A:g0 + 2 * H_A], LANES)], axis=1).astype(BF16)
            bias = jnp.concatenate([_lane_row(b_igate_a[j]), _lane_row(b_fgate_a[j])], axis=1)
            hn = hnorm_a[j].reshape(1, D_INNER)
            w_out = w_out_a[j]

            proj_p, gates_p = _proj(xp, gain, w_main, w_gate, BF16)
            act_p, c_p, n_p, m_p = _mlstm(
                proj_p, gates_p, bias, hn,
                jnp.zeros((bp, H_A, DV_A, DK_A), F32), jnp.zeros((bp, H_A, DK_A), F32),
                jnp.zeros((bp, 1, LANES), F32), bp, tp, BF16)
            proj_s, gates_s = _proj(xs, gain, w_main, w_gate, F32)
            m0 = jnp.pad(state_mlstm_m[j], ((0, 0), (0, LANES - H_A))).reshape(bs, 1, LANES)
            act_s, c_s, n_s, m_s = _mlstm(
                proj_s, gates_s, bias, hn, state_mlstm_c[j], state_mlstm_n[j], m0, bs, ts, F32)
            outs["mc_p"].append(c_p); outs["mn_p"].append(n_p); outs["mm_p"].append(m_p[:, 0, :H_A])
            outs["mc_s"].append(c_s); outs["mn_s"].append(n_s); outs["mm_s"].append(m_s[:, 0, :H_A])
        elif kind == 1:
            w_main = _tile_cols(w_in_b[j])
            w_gate = jnp.zeros((D_MODEL, LANES), BF16)
            w_out = w_out_b[j]
            proj_p, _ = _proj(xp, gain, w_main, w_gate, BF16)
            act_p, st_p = _conv(proj_p, conv_w_b[j], jnp.zeros((bp, CONV_W - 1, D_INNER), F32), bp, tp, BF16)
            proj_s, _ = _proj(xs, gain, w_main, w_gate, F32)
            act_s, st_s = _conv(proj_s, conv_w_b[j], state_conv[j], bs, ts, F32)
            outs["cv_p"].append(st_p); outs["cv_s"].append(st_s)
        else:
            w = w_in_c[j]
            w_main = _tile_cols(jnp.concatenate([w[:, :f0], w[:, f0 + H_C:]], axis=1))
            w_gate = _pad_cols(w[:, f0:f0 + H_C], LANES).astype(BF16)
            bias = _lane_row(b_fgate_c[j])
            w_out = w_out_c[j]

            q_scale = jnp.ones((4 * D_INNER,), F32).at[:D_INNER].set(DH_C ** -0.5 * LOG2E)
            kt = D_INNER // PROJ_TN
            proj_p, gates_p, k_p, v_p = _proj(xp, gain, w_main, w_gate, BF16, col_scale=q_scale,
                                              f32_tiles=((kt, 2 * kt), (2 * kt, 3 * kt)))
            lf_p, _, q_aux, k_aux = _fgate(gates_p, bias, bp, tp, True)
            act_p = _flash(proj_p, q_aux, k_aux, bp, tp)
            outs["k_p"].append(k_p.reshape(bp, tp, H_C, DH_C))
            outs["v_p"].append(v_p.reshape(bp, tp, H_C, DH_C))
            outs["f_p"].append(lf_p[:, :H_C].reshape(bp, tp, H_C))

            proj_s, gates_s = _proj(xs, gain, w_main, w_gate, F32)
            lf_s, c_s = _fgate(gates_s, bias, bs, ts, False)
            k_s = proj_s[:, D_INNER:2 * D_INNER]
            v_s = proj_s[:, 2 * D_INNER:3 * D_INNER]
            c_new = c_s.reshape(bs, ts, LANES)[:, :, :H_C]
            cq_col = c_new.transpose(0, 2, 1).reshape(bs, H_C * ts, 1)
            cn_flat = c_new.reshape(bs, 1, ts * H_C)
            n_pool = cache_k.shape[1]
            pool_k = cache_k[j].reshape(n_pool, PAGE_SIZE * H_C, DH_C)
            pool_v = cache_v[j].reshape(n_pool, PAGE_SIZE * H_C, DH_C)
            pool_lf_t = cache_logf[j].transpose(0, 2, 1)
            act_s = _paged(page_table, proj_s, k_s.reshape(bs, ts * H_C, DH_C), v_s.reshape(bs, ts * H_C, DH_C),
                           cq_col, cn_flat, pool_k, pool_v, pool_lf_t, bs, ts)
            outs["k_s"].append(k_s.reshape(bs, ts, H_C, DH_C))
            outs["v_s"].append(v_s.reshape(bs, ts, H_C, DH_C))
            outs["f_s"].append(lf_s[:, :H_C].reshape(bs, ts, H_C))

        final = i == DEPTH - 1
        w_out = w_out.astype(BF16)
        wpe = w_pe[i].astype(BF16)
        wpg = w_pg[i].astype(BF16)
        pg = pe_norm[i].reshape(1, D_MODEL)
        xp, yp = _out_embed(act_p, xp, p_prompt[i].reshape(bp * tp, PE_DIM), w_out, wpe, wpg, pg, final_gain, final)
        xs, ys = _out_embed(act_s, xs, p_sample[i].reshape(bs * ts, PE_DIM), w_out, wpe, wpg, pg, final_gain, final)

    st = lambda name: jnp.stack(outs[name])
    return (yp.reshape(bp, tp, D_MODEL), ys.reshape(bs, ts, D_MODEL),
            st("mc_p"), st("mn_p"), st("mm_p"), st("mc_s"), st("mn_s"), st("mm_s"),
            st("cv_p"), st("cv_s"),
            st("k_p"), st("v_p"), st("f_p"), st("k_s"), st("v_s"), st("f_s"))
```

```python
import functools

import jax
import jax.numpy as jnp
from jax import lax
from jax.experimental import pallas as pl
from jax.experimental.pallas import tpu as pltpu

F32 = jnp.float32
BF16 = jnp.bfloat16

EPS = 1e-6
DEPTH = 4
N_MIXERS = 3
D_MODEL = 1024
D_INNER = 2 * D_MODEL
H_A = 8
DK_A = D_INNER // (2 * H_A)
DV_A = D_INNER // H_A
CONV_W = 3
DH_C = 128
H_C = D_INNER // DH_C
PE_DIM = 256
PAGE_SIZE = 128
LOG2E = 1.4426950408889634

LANES = 128
SUBLANES = 8
CHUNK = 128
VMEM_LIMIT_BYTES = 48 * 1024 * 1024


def _cparams(n_axes):
    return pltpu.CompilerParams(
        dimension_semantics=("arbitrary",) * n_axes,
        vmem_limit_bytes=VMEM_LIMIT_BYTES)


def _log_sigmoid(x):
    return jnp.minimum(x, 0.0) - jnp.log1p(jnp.exp(-jnp.abs(x)))


def _silu(x):
    return x * jax.nn.sigmoid(x)


def _rms(x, gain):
    return x * lax.rsqrt(jnp.mean(x * x, axis=-1, keepdims=True) + EPS) * gain


def _split3(x):
    hi = x.astype(BF16)
    r1 = x - hi.astype(F32)
    mid = r1.astype(BF16)
    lo = (r1 - mid.astype(F32)).astype(BF16)
    return hi, mid, lo


def _pad_rows(x, rows):
    if x.shape[0] == rows:
        return x
    return jnp.concatenate([x, jnp.zeros((rows - x.shape[0], x.shape[1]), x.dtype)], axis=0)


PROJ_TN = 1024


def _proj_kernel(x_ref, g_ref, w_ref, wg_ref, cs_ref, o_ref, og_ref, *rest, f32_tiles, scaled):
    extra_refs, h_scr = rest[:-1], rest[-1]
    j = pl.program_id(1)

    @pl.when(j == 0)
    def _():
        h = _rms(x_ref[...], g_ref[...]).astype(BF16)
        h_scr[...] = h
        og_ref[...] = jnp.dot(h, wg_ref[...], preferred_element_type=F32)

    res = jnp.dot(h_scr[...], w_ref[j], preferred_element_type=F32)
    if scaled:
        res = res * cs_ref[j]
    o_ref[...] = res.astype(o_ref.dtype)
    for ref, (lo, hi) in zip(extra_refs, f32_tiles):
        @pl.when((j >= lo) & (j < hi))
        def _(ref=ref):
            ref[...] = res


def _proj(x, gain, w_tiles, wg, out_dtype, col_scale=None, f32_tiles=()):
    m, d = x.shape
    nt, _, tn = w_tiles.shape
    ng = wg.shape[1]
    tm = min(512, m)
    scaled = col_scale is not None
    cs = (col_scale if scaled else jnp.ones((nt * tn,), F32)).reshape(nt, 1, tn)
    resident = lambda shape: pl.BlockSpec(shape, lambda i, j: (0,) * len(shape), pipeline_mode=pl.Buffered(1))
    extra_specs = [pl.BlockSpec((tm, tn), lambda i, j, lo=lo, hi=hi: (i, jnp.clip(j - lo, 0, hi - lo - 1)))
                   for lo, hi in f32_tiles]
    extra_shapes = [jax.ShapeDtypeStruct((m, (hi - lo) * tn), F32) for lo, hi in f32_tiles]
    kern = functools.partial(_proj_kernel, f32_tiles=tuple(f32_tiles), scaled=scaled)
    return pl.pallas_call(
        kern,
        grid=(m // tm, nt),
        in_specs=[
            pl.BlockSpec((tm, d), lambda i, j: (i, 0)),
            pl.BlockSpec((1, d), lambda i, j: (0, 0)),
            resident((nt, d, tn)),
            resident((d, ng)),
            resident((nt, 1, tn)),
        ],
        out_specs=[
            pl.BlockSpec((tm, tn), lambda i, j: (i, j)),
            pl.BlockSpec((tm, ng), lambda i, j: (i, 0)),
        ] + extra_specs,
        out_shape=[
            jax.ShapeDtypeStruct((m, nt * tn), out_dtype),
            jax.ShapeDtypeStruct((m, ng), F32),
        ] + extra_shapes,
        scratch_shapes=[pltpu.VMEM((tm, d), BF16)],
        compiler_params=_cparams(2),
        name="norm_proj",
    )(x, gain, w_tiles, wg, cs)


def _tile_cols(w):
    d, n = w.shape
    return w.reshape(d, n // PROJ_TN, PROJ_TN).transpose(1, 0, 2).astype(BF16)


def _out_embed_kernel(act_ref, x_ref, p_ref, wo_ref, wpe_ref, wpg_ref, g_ref, fg_ref,
                      xo_ref, *maybe_y_ref):
    y = jnp.dot(act_ref[...].astype(BF16), wo_ref[...], preferred_element_type=F32)
    x1 = x_ref[...] + y
    rn = _rms(x1, g_ref[...]).astype(BF16)
    gate = jax.nn.sigmoid(jnp.dot(rn, wpg_ref[...], preferred_element_type=F32))
    pe = jnp.dot(p_ref[...].astype(BF16), wpe_ref[...], preferred_element_type=F32)
    x2 = x1 + pe * gate
    xo_ref[...] = x2
    if maybe_y_ref:
        maybe_y_ref[0][...] = _rms(x2, fg_ref[...])


def _out_embed(act, x, p, wo, wpe, wpg, gain, final_gain, final):
    m, d = x.shape
    di = act.shape[1]
    pe = p.shape[1]
    tm = min(512, m)
    row = lambda i: (i, 0)
    const = lambda i: (0, 0)
    n_out = 2 if final else 1
    outs = pl.pallas_call(
        _out_embed_kernel,
        grid=(m // tm,),
        in_specs=[
            pl.BlockSpec((tm, di), row),
            pl.BlockSpec((tm, d), row),
            pl.BlockSpec((tm, pe), row),
            pl.BlockSpec((di, d), const),
            pl.BlockSpec((pe, d), const),
            pl.BlockSpec((d, d), const),
            pl.BlockSpec((1, d), const),
            pl.BlockSpec((1, d), const),
        ],
        out_specs=[pl.BlockSpec((tm, d), row)] * n_out,
        out_shape=[jax.ShapeDtypeStruct((m, d), F32)] * n_out,
        compiler_params=_cparams(1),
        name="out_embed",
    )(act, x, p, wo, wpe, wpg, gain, final_gain)
    return outs if final else (outs[0], None)


def _mlstm_kernel(q_ref, k_ref, v_ref, o_ref, z_ref, g_ref, bias_ref, hn_ref,
                  c0_ref, n0_ref, m0_ref,
                  act_ref, c_out, n_out, m_out,
                  ct_scr, n_scr, m_scr, *, lv):
    c = pl.program_id(1)
    L = CHUNK
    scale = DK_A ** -0.5

    @pl.when(c == 0)
    def _():
        for h in range(H_A):
            ct_scr[h] = c0_ref[0, h].T
        n_scr[...] = n0_ref[0]
        m_scr[...] = m0_ref[0]

    q = _pad_rows(q_ref[...], L).astype(BF16)
    k = _pad_rows(k_ref[...], L).astype(BF16)
    v = _pad_rows(v_ref[...], L).astype(BF16)
    g = _pad_rows(g_ref[...], L) + bias_ref[...]
    li = g[:, :LANES]
    lf = _log_sigmoid(g[:, LANES:])
    if lv < L:
        valid = lax.broadcasted_iota(jnp.int32, (L, LANES), 0) < lv
        li = jnp.where(valid, li, -jnp.inf)
        lf = jnp.where(valid, lf, 0.0)

    row = lax.broadcasted_iota(jnp.int32, (L, L), 0)
    col = lax.broadcasted_iota(jnp.int32, (L, L), 1)
    causal = col <= row
    tri = jnp.where(causal, 1.0, 0.0).astype(BF16)
    b = sum(jnp.dot(tri, piece, preferred_element_type=F32) for piece in _split3(lf))
    r = li - b
    r_t = r.T
    m_row = m_scr[...]
    b_last = b[L - 1:L, :]
    m_new = jnp.maximum(b_last + m_row, jnp.max(b_last + r, axis=0, keepdims=True))
    decay = jnp.exp(b_last + m_row - m_new)
    ws_t = jnp.exp(b_last + r - m_new).T
    n_all = n_scr[...]
    ones_bf16 = jnp.ones((L, LANES), BF16)

    for h in range(H_A):
        qh = q[:, h * DK_A:(h + 1) * DK_A]
        kh = k[:, h * DK_A:(h + 1) * DK_A]
        vh = v[:, h * DV_A:(h + 1) * DV_A]
        wide = lambda x: jnp.concatenate([x] * (DV_A // LANES), axis=1)
        m0h = m_row[:, h:h + 1]
        rmat = jnp.where(causal, r_t[h:h + 1, :], -jnp.inf)
        mx = jnp.broadcast_to(jnp.maximum(jnp.max(rmat, axis=1, keepdims=True), m0h), (L, LANES))
        w_intra = jnp.exp(rmat - mx)
        w_inter = jnp.exp(m0h - mx) * scale
        qk = lax.dot_general(qh, kh, (((1,), (1,)), ((), ())), preferred_element_type=F32)
        s = qk * scale * w_intra
        s_hi = s.astype(BF16)
        s_lo = (s - s_hi.astype(F32)).astype(BF16)
        ct = ct_scr[h]
        num = (jnp.dot(s_hi, vh, preferred_element_type=F32)
               + wide(w_inter) * jnp.dot(qh, ct.astype(BF16), preferred_element_type=F32))
        n_rows = jnp.broadcast_to(n_all[h:h + 1, :], (LANES, DK_A)).astype(BF16)
        qn = lax.dot_general(qh, n_rows, (((1,), (1,)), ((), ())), preferred_element_type=F32)
        row_sum = (jnp.dot(s_hi, ones_bf16, preferred_element_type=F32)
                   + jnp.dot(s_lo, ones_bf16, preferred_element_type=F32))
        den = row_sum + w_inter * qn
        floor = jnp.exp(-(b[:, h:h + 1] + mx))
        hh = num * wide(1.0 / jnp.maximum(jnp.abs(den), floor))
        msq = jnp.broadcast_to(jnp.mean(hh * hh, axis=1, keepdims=True), (L, LANES))
        hh = hh * wide(lax.rsqrt(msq + EPS))
        hh = hh * hn_ref[:, h * DV_A:(h + 1) * DV_A]
        oh = o_ref[:, h * DV_A:(h + 1) * DV_A].astype(F32)
        zh = z_ref[:, h * DV_A:(h + 1) * DV_A].astype(F32)
        y = hh[:lv] * zh * (1.0 / ((1.0 + jnp.exp(-oh)) * (1.0 + jnp.exp(-zh))))
        act_ref[:, h * DV_A:(h + 1) * DV_A] = y.astype(act_ref.dtype)

        ws_row = ws_t[h:h + 1, :]
        kts = (kh.astype(F32).T * ws_row).astype(BF16)
        dh = decay[:, h:h + 1]
        ct_scr[h] = dh * ct + jnp.dot(kts, vh, preferred_element_type=F32)
        ws8 = jnp.broadcast_to(ws_row, (SUBLANES, L)).astype(BF16)
        n_scr[h:h + 1, :] = dh * n_all[h:h + 1, :] + jnp.dot(ws8, kh, preferred_element_type=F32)[0:1]
    m_scr[...] = m_new

    @pl.when(c == pl.num_programs(1) - 1)
    def _():
        for h in range(H_A):
            c_out[0, h] = ct_scr[h].T
        n_out[0] = n_scr[...]
        m_out[0] = m_scr[...]


def _mlstm(proj, gates, bias, hnorm, c0, n0, m0, bsz, t, act_dtype):
    lv = min(CHUNK, t)
    nc = t // lv
    m = bsz * t
    rows = lambda b, c: b * nc + c
    kern = functools.partial(_mlstm_kernel, lv=lv)
    return pl.pallas_call(
        kern,
        grid=(bsz, nc),
        in_specs=[
            pl.BlockSpec((lv, H_A * DK_A), lambda b, c: (rows(b, c), 0)),
            pl.BlockSpec((lv, H_A * DK_A), lambda b, c: (rows(b, c), 1)),
            pl.BlockSpec((lv, D_INNER), lambda b, c: (rows(b, c), 1)),
            pl.BlockSpec((lv, D_INNER), lambda b, c: (rows(b, c), 2)),
            pl.BlockSpec((lv, D_INNER), lambda b, c: (rows(b, c), 3)),
            pl.BlockSpec((lv, 2 * LANES), lambda b, c: (rows(b, c), 0)),
            pl.BlockSpec((1, 2 * LANES), lambda b, c: (0, 0)),
            pl.BlockSpec((1, D_INNER), lambda b, c: (0, 0)),
            pl.BlockSpec((1, H_A, DV_A, DK_A), lambda b, c: (b, 0, 0, 0)),
            pl.BlockSpec((1, H_A, DK_A), lambda b, c: (b, 0, 0)),
            pl.BlockSpec((1, 1, LANES), lambda b, c: (b, 0, 0)),
        ],
        out_specs=[
            pl.BlockSpec((lv, D_INNER), lambda b, c: (rows(b, c), 0)),
            pl.BlockSpec((1, H_A, DV_A, DK_A), lambda b, c: (b, 0, 0, 0)),
            pl.BlockSpec((1, H_A, DK_A), lambda b, c: (b, 0, 0)),
            pl.BlockSpec((1, 1, LANES), lambda b, c: (b, 0, 0)),
        ],
        out_shape=[
            jax.ShapeDtypeStruct((m, D_INNER), act_dtype),
            jax.ShapeDtypeStruct((bsz, H_A, DV_A, DK_A), F32),
            jax.ShapeDtypeStruct((bsz, H_A, DK_A), F32),
            jax.ShapeDtypeStruct((bsz, 1, LANES), F32),
        ],
        scratch_shapes=[
            pltpu.VMEM((H_A, DK_A, DV_A), F32),
            pltpu.VMEM((H_A, DK_A), F32),
            pltpu.VMEM((1, LANES), F32),
        ],
        compiler_params=_cparams(2),
        name="mlstm_mixer",
    )(proj, proj, proj, proj, proj, gates, bias, hnorm, c0, n0, m0)


def _gated_conv(bg, cg, xin, z, cw, prev):
    u = cg * xin
    p2 = prev[0:1, :]
    p1 = prev[1:2, :]
    row = lax.broadcasted_iota(jnp.int32, u.shape, 0)
    u1 = jnp.where(row == 0, p1, pltpu.roll(u, 1, axis=0))
    u2 = jnp.where(row == 0, p2, jnp.where(row == 1, p1, pltpu.roll(u, 2, axis=0)))
    conv = cw[0:1, :] * u2 + cw[1:2, :] * u1 + cw[2:3, :] * u
    return bg * conv * _silu(z), u[u.shape[0] - (CONV_W - 1):, :]


def _conv_kernel(bg_ref, cg_ref, xin_ref, z_ref, cw_ref, st_ref, act_ref, st_out, carry):
    @pl.when(pl.program_id(1) == 0)
    def _():
        carry[...] = st_ref[0]

    y, last = _gated_conv(bg_ref[...].astype(F32), cg_ref[...].astype(F32), xin_ref[...].astype(F32),
                          z_ref[...].astype(F32), cw_ref[...], carry[...])
    act_ref[...] = y.astype(act_ref.dtype)
    carry[...] = last
    st_out[0] = last


def _conv(proj, conv_w, state, bsz, t, act_dtype):
    tt = min(256, t)
    nt = t // tt
    m = bsz * t
    blk = lambda j: pl.BlockSpec((tt, D_INNER), lambda b, i: (b * nt + i, j))
    return pl.pallas_call(
        _conv_kernel,
        grid=(bsz, nt),
        in_specs=[
            blk(0), blk(1), blk(2), blk(3),
            pl.BlockSpec((CONV_W, D_INNER), lambda b, i: (0, 0)),
            pl.BlockSpec((1, CONV_W - 1, D_INNER), lambda b, i: (b, 0, 0)),
        ],
        out_specs=[
            pl.BlockSpec((tt, D_INNER), lambda b, i: (b * nt + i, 0)),
            pl.BlockSpec((1, CONV_W - 1, D_INNER), lambda b, i: (b, 0, 0)),
        ],
        out_shape=[
            jax.ShapeDtypeStruct((m, D_INNER), act_dtype),
            jax.ShapeDtypeStruct((bsz, CONV_W - 1, D_INNER), F32),
        ],
        scratch_shapes=[pltpu.VMEM((CONV_W - 1, D_INNER), F32)],
        compiler_params=_cparams(2),
        name="conv_mixer",
    )(proj, proj, proj, proj, conv_w, state)


def _proj_conv_kernel(x_ref, g_ref, w_ref, cw_ref, st_ref, act_ref, st_out, h_scr, carry, *, tiles_per_seq):
    i = pl.program_id(0)
    j = pl.program_id(1)
    cs = act_ref.shape[1]

    @pl.when(j == 0)
    def _():
        h_scr[...] = _rms(x_ref[...], g_ref[...]).astype(BF16)

    res = jnp.dot(h_scr[...], w_ref[j], preferred_element_type=F32)
    prev = jnp.where(i % tiles_per_seq == 0, st_ref[0], carry[j])
    y, last = _gated_conv(res[:, :cs], res[:, cs:2 * cs], res[:, 2 * cs:3 * cs], res[:, 3 * cs:],
                          cw_ref[j], prev)
    act_ref[...] = y.astype(act_ref.dtype)
    carry[j] = last
    st_out[0] = last


def _proj_conv(x, gain, w, conv_w, state, bsz, t):
    m, d = x.shape
    tm = 512
    assert t % tm == 0
    cs = PROJ_TN // 4
    nt = D_INNER // cs
    w_tiles = w.reshape(d, 4, nt, cs).transpose(2, 0, 1, 3).reshape(nt, d, 4 * cs).astype(BF16)
    cw_tiles = conv_w.reshape(CONV_W, nt, cs).transpose(1, 0, 2)
    tiles_per_seq = t // tm
    resident = lambda shape: pl.BlockSpec(shape, lambda i, j: (0,) * len(shape), pipeline_mode=pl.Buffered(1))
    state_spec = pl.BlockSpec((1, CONV_W - 1, cs), lambda i, j: (i // tiles_per_seq, 0, j))
    tile_state_spec = pl.BlockSpec((1, CONV_W - 1, cs), lambda i, j: (i, 0, j))
    act, tile_states = pl.pallas_call(
        functools.partial(_proj_conv_kernel, tiles_per_seq=tiles_per_seq),
        grid=(m // tm, nt),
        in_specs=[
            pl.BlockSpec((tm, d), lambda i, j: (i, 0)),
            pl.BlockSpec((1, d), lambda i, j: (0, 0)),
            resident((nt, d, 4 * cs)),
            resident((nt, CONV_W, cs)),
            state_spec,
        ],
        out_specs=[pl.BlockSpec((tm, cs), lambda i, j: (i, j)), tile_state_spec],
        out_shape=[
            jax.ShapeDtypeStruct((m, D_INNER), BF16),
            jax.ShapeDtypeStruct((m // tm, CONV_W - 1, D_INNER), F32),
        ],
        scratch_shapes=[pltpu.VMEM((tm, d), BF16), pltpu.VMEM((nt, CONV_W - 1, cs), F32)],
        compiler_params=_cparams(2),
        name="norm_proj_conv",
    )(x, gain, w_tiles, cw_tiles, state)
    return act, tile_states[tiles_per_seq - 1::tiles_per_seq]


AUX_K0 = 3 * H_C


def _fgate_kernel(g_ref, bias_ref, lf_ref, c_ref, *rest, lv, with_aux):
    carry = rest[-1]
    L = CHUNK

    @pl.when(pl.program_id(1) == 0)
    def _():
        carry[...] = jnp.zeros_like(carry)

    lf = _log_sigmoid(g_ref[...] + bias_ref[...])
    lf_ref[...] = lf
    lf = _pad_rows(lf, L)
    if lv < L:
        lf = jnp.where(lax.broadcasted_iota(jnp.int32, (L, LANES), 0) < lv, lf, 0.0)
    row = lax.broadcasted_iota(jnp.int32, (L, L), 0)
    col = lax.broadcasted_iota(jnp.int32, (L, L), 1)
    tri = jnp.where(col <= row, 1.0, 0.0).astype(BF16)
    cs = sum(jnp.dot(tri, piece, preferred_element_type=F32) for piece in _split3(lf)) + carry[...]
    c_ref[...] = cs[:lv]
    carry[...] = cs[L - 1:L, :]
    if with_aux:
        qa_ref, ka_ref = rest[0], rest[1]
        qa = jnp.zeros((L, LANES), F32)
        ka = jnp.zeros((L, LANES), F32)
        head_ok = row < H_C
        for j, piece in enumerate(_split3(cs * LOG2E)):
            to_q = jnp.where(head_ok & (col == 3 * row + j), 1.0, 0.0).astype(BF16)
            to_k = jnp.where(head_ok & (col == AUX_K0 + 3 * row + j), 1.0, 0.0).astype(BF16)
            qa = qa + jnp.dot(piece, to_q, preferred_element_type=F32)
            ka = ka - jnp.dot(piece, to_k, preferred_element_type=F32)
        qa_ref[...] = qa.astype(BF16)
        ka_ref[...] = ka.astype(BF16)


def _fgate(gates, bias, bsz, t, with_aux):
    lv = min(CHUNK, t)
    nc = t // lv
    m = bsz * t
    rows = pl.BlockSpec((lv, LANES), lambda b, c: (b * nc + c, 0))
    aux_spec = rows
    aux_shape = jax.ShapeDtypeStruct((m, LANES), BF16)
    kern = functools.partial(_fgate_kernel, lv=lv, with_aux=with_aux)
    return pl.pallas_call(
        kern,
        grid=(bsz, nc),
        in_specs=[rows, pl.BlockSpec((1, LANES), lambda b, c: (0, 0))],
        out_specs=[rows, rows] + [aux_spec] * (2 * with_aux),
        out_shape=[jax.ShapeDtypeStruct((m, LANES), F32)] * 2 + [aux_shape] * (2 * with_aux),
        scratch_shapes=[pltpu.VMEM((1, LANES), F32)],
        compiler_params=_cparams(2),
        name="fgate_scan",
    )(gates, bias)


def _online_softmax_step(s_blocks, m, l, acc, values):
    bmax = functools.reduce(jnp.maximum, s_blocks)
    m_new = jnp.maximum(m, jnp.max(bmax, axis=1, keepdims=True))
    corr = jnp.exp2(m - m_new)
    ps = [jnp.exp2(blk - m_new) for blk in s_blocks]
    l_new = l * corr + functools.reduce(jnp.add, ps)
    p = ps[0] if len(ps) == 1 else jnp.concatenate(ps, axis=1)
    acc_new = acc * corr + jnp.dot(p.astype(BF16), values, preferred_element_type=F32)
    return m_new, l_new, acc_new


def _lane_blocks(s):
    return [s[:, b * LANES:(b + 1) * LANES] for b in range(s.shape[1] // LANES)]


FLASH_T = 512


def _flash_kernel(q_ref, k_ref, v_ref, z_ref, qa_ref, ka_ref, o_ref, *, tq):
    nq = q_ref.shape[0] // tq
    causal = (lax.broadcasted_iota(jnp.int32, (tq, tq), 1)
              <= lax.broadcasted_iota(jnp.int32, (tq, tq), 0))
    lane = lax.broadcasted_iota(jnp.int32, (1, LANES), 1) - 3 * pl.program_id(1)
    ones_q = jnp.where((lane >= AUX_K0) & (lane < AUX_K0 + 3), 1.0, 0.0).astype(BF16)
    ones_k = jnp.where((lane >= 0) & (lane < 3), 1.0, 0.0).astype(BF16)
    for i in range(nq):
        rs = slice(i * tq, (i + 1) * tq)
        q_aug = jnp.concatenate([q_ref[rs, :], qa_ref[rs, :] + ones_q], axis=1)
        m = jnp.full((tq, LANES), -jnp.inf, F32)
        l = jnp.zeros((tq, LANES), F32)
        acc = jnp.zeros((tq, DH_C), F32)
        for j in range(i + 1):
            ks = slice(j * tq, (j + 1) * tq)
            k_aug = jnp.concatenate([k_ref[ks, :], ka_ref[ks, :] + ones_k], axis=1)
            s = lax.dot_general(q_aug, k_aug, (((1,), (1,)), ((), ())), preferred_element_type=F32)
            if j == i:
                s = jnp.where(causal, s, -jnp.inf)
            m, l, acc = _online_softmax_step(_lane_blocks(s), m, l, acc, v_ref[ks, :])
        o = acc * (1.0 / jnp.sum(l, axis=1, keepdims=True))
        o_ref[rs, :] = (o * _silu(z_ref[rs, :].astype(F32))).astype(o_ref.dtype)


def _flash(proj, q_aux, k_aux, bsz, t):
    tq = min(FLASH_T, t)
    col = lambda j: pl.BlockSpec((t, DH_C), lambda b, h: (b, j * H_C + h))
    aux = pl.BlockSpec((t, LANES), lambda b, h: (b, 0))
    return pl.pallas_call(
        functools.partial(_flash_kernel, tq=tq),
        grid=(bsz, H_C),
        in_specs=[col(0), col(1), col(2), col(3), aux, aux],
        out_specs=pl.BlockSpec((t, DH_C), lambda b, h: (b, h)),
        out_shape=jax.ShapeDtypeStruct((bsz * t, D_INNER), BF16),
        compiler_params=_cparams(2),
        name="fox_flash",
    )(proj, proj, proj, proj, q_aux, k_aux)


PAGES_PER_STEP = 8


def _paged_kernel(pt_ref, q_ref, kn_ref, vn_ref, z_ref, cqc_ref, cnf_ref, *rest, t):
    page_refs = rest[:3 * PAGES_PER_STEP]
    out_ref, qs_scr, wq_scr, mask_scr, spread_scr, m_scr, l_scr, acc_scr, carry = rest[3 * PAGES_PER_STEP:]
    p = pl.program_id(1)
    P = PAGE_SIZE
    R = H_C * t
    W = P * H_C
    t_shift = t.bit_length() - 1
    h_shift = H_C.bit_length() - 1

    @pl.when(p == 0)
    def _():
        m_scr[...] = jnp.full_like(m_scr, -jnp.inf)
        l_scr[...] = jnp.zeros_like(l_scr)
        acc_scr[...] = jnp.zeros_like(acc_scr)
        carry[...] = jnp.zeros_like(carry)
        q = q_ref[...] * (DH_C ** -0.5 * LOG2E)
        q_rows = jnp.concatenate([q[:, h * DH_C:(h + 1) * DH_C] for h in range(H_C)], axis=0)
        qs_scr[...] = q_rows.astype(BF16)
        q_t = q_rows.T.astype(BF16)
        zero = jnp.zeros_like(q_t)
        wq_scr[...] = jnp.concatenate([jnp.concatenate([q_t, zero], axis=1),
                                       jnp.concatenate([zero, q_t], axis=1)], axis=0)
        row_h = lax.shift_right_logical(lax.broadcasted_iota(jnp.int32, (R, W), 0), t_shift)
        col_h = lax.broadcasted_iota(jnp.int32, (R, W), 1) & (H_C - 1)
        mask_scr[...] = jnp.where(row_h == col_h, cqc_ref[0] * LOG2E, -jnp.inf)
        pos = lax.broadcasted_iota(jnp.int32, (P, W), 0)
        col_p = lax.shift_right_logical(lax.broadcasted_iota(jnp.int32, (P, W), 1), h_shift)
        spread_scr[...] = jnp.where(pos > col_p, 1.0, 0.0).astype(BF16)

    qs = qs_scr[...]
    head = lax.broadcasted_iota(jnp.int32, (H_C, W), 0)
    col_h16 = lax.broadcasted_iota(jnp.int32, (H_C, W), 1) & (H_C - 1)
    state = (m_scr[...], l_scr[...], acc_scr[...])
    suffix_carry = carry[...]
    pages = list(zip(*[page_refs[i::3] for i in range(3)]))
    pieces = jnp.concatenate([piece for _, _, lfp_ref in pages for piece in _split3(lfp_ref[0])], axis=0)
    g_all = jnp.dot(pieces, spread_scr[...], preferred_element_type=F32)
    half = W // 2
    for i, (kp_ref, vp_ref, lfp_ref) in enumerate(pages):
        g = sum(g_all[(3 * i + j) * H_C:(3 * i + j + 1) * H_C, :] for j in range(3)) + suffix_carry
        bias = jnp.sum(jnp.where(head == col_h16, g, 0.0), axis=0, keepdims=True) * LOG2E
        suffix_carry = suffix_carry + jnp.sum(lfp_ref[0], axis=1, keepdims=True)
        keys = jnp.concatenate([kp_ref[0, :half, :].astype(BF16), kp_ref[0, half:, :].astype(BF16)], axis=1)
        st = jnp.dot(keys, wq_scr[...], preferred_element_type=F32)
        s = jnp.concatenate([st[:, :R].T, st[:, R:].T], axis=1)
        state = _online_softmax_step(_lane_blocks(s + mask_scr[...] + bias), *state, vp_ref[0].astype(BF16))
    carry[...] = suffix_carry
    m_scr[...], l_scr[...], acc_scr[...] = state

    @pl.when(p == pl.num_programs(1) - 1)
    def _():
        sn = lax.dot_general(qs, kn_ref[0].astype(BF16), (((1,), (1,)), ((), ())), preferred_element_type=F32)
        row = lax.broadcasted_iota(jnp.int32, (R, R), 0)
        col = lax.broadcasted_iota(jnp.int32, (R, R), 1)
        valid = (((col & (H_C - 1)) == lax.shift_right_logical(row, t_shift))
                 & (lax.shift_right_logical(col, h_shift) <= (row & (t - 1))))
        sn = jnp.where(valid, sn + (cqc_ref[0] - cnf_ref[0]) * LOG2E, -jnp.inf)
        _, l, acc = _online_softmax_step([sn], m_scr[...], l_scr[...], acc_scr[...], vn_ref[0].astype(BF16))
        o = acc * (1.0 / jnp.sum(l, axis=1, keepdims=True))
        for h in range(H_C):
            zh = z_ref[:, h * DH_C:(h + 1) * DH_C]
            out_ref[:, h * DH_C:(h + 1) * DH_C] = (o[h * t:(h + 1) * t, :] * _silu(zh)).astype(out_ref.dtype)


def _paged(page_table, proj, kn_rows, vn_rows, cq_col, cn_flat, pool_k, pool_v, pool_lf_t, bsz, t):
    assert t & (t - 1) == 0 and t * H_C == LANES, "sample length must be a power of two with t*H == 128"
    n_pages = page_table.shape[1]
    g = PAGES_PER_STEP
    assert n_pages % g == 0
    R = H_C * t
    W = PAGE_SIZE * H_C
    per_b = lambda b, p, pt: (b, 0, 0)
    page_specs = []
    for i in range(g):
        page = lambda b, p, pt, i=i: (pt[b, n_pages - 1 - (p * g + i)], 0, 0)
        page_specs += [pl.BlockSpec((1, W, DH_C), page), pl.BlockSpec((1, W, DH_C), page),
                       pl.BlockSpec((1, H_C, PAGE_SIZE), page)]
    grid_spec = pltpu.PrefetchScalarGridSpec(
        num_scalar_prefetch=1,
        grid=(bsz, n_pages // g),
        in_specs=[
            pl.BlockSpec((t, D_INNER), lambda b, p, pt: (b, 0)),
            pl.BlockSpec((1, R, DH_C), per_b),
            pl.BlockSpec((1, R, DH_C), per_b),
            pl.BlockSpec((t, D_INNER), lambda b, p, pt: (b, 3)),
            pl.BlockSpec((1, R, 1), per_b),
            pl.BlockSpec((1, 1, R), per_b),
        ] + page_specs,
        out_specs=pl.BlockSpec((t, D_INNER), lambda b, p, pt: (b, 0)),
        scratch_shapes=[
            pltpu.VMEM((R, DH_C), BF16),
            pltpu.VMEM((2 * DH_C, 2 * R), BF16),
            pltpu.VMEM((R, W), F32),
            pltpu.VMEM((PAGE_SIZE, W), BF16),
            pltpu.VMEM((R, LANES), F32),
            pltpu.VMEM((R, LANES), F32),
            pltpu.VMEM((R, DH_C), F32),
            pltpu.VMEM((H_C, 1), F32),
        ],
    )
    return pl.pallas_call(
        functools.partial(_paged_kernel, t=t),
        grid_spec=grid_spec,
        out_shape=jax.ShapeDtypeStruct((bsz * t, D_INNER), F32),
        compiler_params=_cparams(2),
        name="fox_paged",
    )(page_table, proj, kn_rows, vn_rows, proj, cq_col, cn_flat, *([pool_k, pool_v, pool_lf_t] * g))


def _pad_cols(w, n):
    return jnp.pad(w, ((0, 0), (0, n - w.shape[1])))


def _lane_row(vec, offset=0, width=LANES):
    return jnp.zeros((1, width), F32).at[0, offset:offset + vec.shape[0]].set(vec)


def kernel(x_prompt, x_sample, state_mlstm_c, state_mlstm_n, state_mlstm_m, state_conv, cache_k, cache_v, cache_logf, page_table, p_prompt, p_sample, ln_gain, w_in_a, b_igate_a, b_fgate_a, hnorm_a, w_out_a, w_in_b, conv_w_b, w_out_b, w_in_c, b_fgate_c, w_out_c, w_pe, w_pg, pe_norm, final_norm):
    bp, tp, _ = x_prompt.shape
    bs, ts, _ = x_sample.shape
    xp = x_prompt.reshape(bp * tp, D_MODEL)
    xs = x_sample.reshape(bs * ts, D_MODEL)
    final_gain = final_norm.reshape(1, D_MODEL)
    qk = H_A * DK_A
    g0 = 2 * qk + D_INNER
    f0 = 3 * D_INNER

    outs = {name: [] for name in (
        "mc_p", "mn_p", "mm_p", "mc_s", "mn_s", "mm_s", "cv_p", "cv_s",
        "k_p", "v_p", "f_p", "k_s", "v_s", "f_s")}
    yp = ys = None
    for i in range(DEPTH):
        j = i // N_MIXERS
        gain = ln_gain[i].reshape(1, D_MODEL)
        kind = i % N_MIXERS
        if kind == 0:
            w = w_in_a[j]
            w_main = _tile_cols(jnp.concatenate([w[:, :g0], w[:, g0 + 2 * H_A:]], axis=1))
            w_gate = jnp.concatenate([_pad_cols(w[:, g0:g0 + H_A], LANES),
                                      _pad_cols(w[:, g0 + H_A:g0 + 2 * H_A], LANES)], axis=1).astype(BF16)
            bias = jnp.concatenate([_lane_row(b_igate_a[j]), _lane_row(b_fgate_a[j])], axis=1)
            hn = hnorm_a[j].reshape(1, D_INNER)
            w_out = w_out_a[j]

            proj_p, gates_p = _proj(xp, gain, w_main, w_gate, BF16)
            act_p, c_p, n_p, m_p = _mlstm(
                proj_p, gates_p, bias, hn,
                jnp.zeros((bp, H_A, DV_A, DK_A), F32), jnp.zeros((bp, H_A, DK_A), F32),
                jnp.zeros((bp, 1, LANES), F32), bp, tp, BF16)
            proj_s, gates_s = _proj(xs, gain, w_main, w_gate, F32)
            m0 = jnp.pad(state_mlstm_m[j], ((0, 0), (0, LANES - H_A))).reshape(bs, 1, LANES)
            act_s, c_s, n_s, m_s = _mlstm(
                proj_s, gates_s, bias, hn, state_mlstm_c[j], state_mlstm_n[j], m0, bs, ts, F32)
            outs["mc_p"].append(c_p); outs["mn_p"].append(n_p); outs["mm_p"].append(m_p[:, 0, :H_A])
            outs["mc_s"].append(c_s); outs["mn_s"].append(n_s); outs["mm_s"].append(m_s[:, 0, :H_A])
        elif kind == 1:
            w_main = _tile_cols(w_in_b[j])
            w_gate = jnp.zeros((D_MODEL, LANES), BF16)
            w_out = w_out_b[j]
            act_p, st_p = _proj_conv(xp, gain, w_in_b[j], conv_w_b[j],
                                     jnp.zeros((bp, CONV_W - 1, D_INNER), F32), bp, tp)
            proj_s, _ = _proj(xs, gain, w_main, w_gate, F32)
            act_s, st_s = _conv(proj_s, conv_w_b[j], state_conv[j], bs, ts, F32)
            outs["cv_p"].append(st_p); outs["cv_s"].append(st_s)
        else:
            w = w_in_c[j]
            w_main = _tile_cols(jnp.concatenate([w[:, :f0], w[:, f0 + H_C:]], axis=1))
            w_gate = _pad_cols(w[:, f0:f0 + H_C], LANES).astype(BF16)
            bias = _lane_row(b_fgate_c[j])
            w_out = w_out_c[j]

            q_scale = jnp.ones((4 * D_INNER,), F32).at[:D_INNER].set(DH_C ** -0.5 * LOG2E)
            kt = D_INNER // PROJ_TN
            proj_p, gates_p, k_p, v_p = _proj(xp, gain, w_main, w_gate, BF16, col_scale=q_scale,
                                              f32_tiles=((kt, 2 * kt), (2 * kt, 3 * kt)))
            lf_p, _, q_aux, k_aux = _fgate(gates_p, bias, bp, tp, True)
            act_p = _flash(proj_p, q_aux, k_aux, bp, tp)
            outs["k_p"].append(k_p.reshape(bp, tp, H_C, DH_C))
            outs["v_p"].append(v_p.reshape(bp, tp, H_C, DH_C))
            outs["f_p"].append(lf_p[:, :H_C].reshape(bp, tp, H_C))

            proj_s, gates_s = _proj(xs, gain, w_main, w_gate, F32)
            lf_s, c_s = _fgate(gates_s, bias, bs, ts, False)
            k_s = proj_s[:, D_INNER:2 * D_INNER]
            v_s = proj_s[:, 2 * D_INNER:3 * D_INNER]
            c_new = c_s.reshape(bs, ts, LANES)[:, :, :H_C]
            cq_col = c_new.transpose(0, 2, 1).reshape(bs, H_C * ts, 1)
            cn_flat = c_new.reshape(bs, 1, ts * H_C)
            n_pool = cache_k.shape[1]
            pool_k = cache_k[j].reshape(n_pool, PAGE_SIZE * H_C, DH_C)
            pool_v = cache_v[j].reshape(n_pool, PAGE_SIZE * H_C, DH_C)
            pool_lf_t = cache_logf[j].transpose(0, 2, 1)
            act_s = _paged(page_table, proj_s, k_s.reshape(bs, ts * H_C, DH_C), v_s.reshape(bs, ts * H_C, DH_C),
                           cq_col, cn_flat, pool_k, pool_v, pool_lf_t, bs, ts)
            outs["k_s"].append(k_s.reshape(bs, ts, H_C, DH_C))
            outs["v_s"].append(v_s.reshape(bs, ts, H_C, DH_C))
            outs["f_s"].append(lf_s[:, :H_C].reshape(bs, ts, H_C))

        final = i == DEPTH - 1
        w_out = w_out.astype(BF16)
        wpe = w_pe[i].astype(BF16)
        wpg = w_pg[i].astype(BF16)
        pg = pe_norm[i].reshape(1, D_MODEL)
        xp, yp = _out_embed(act_p, xp, p_prompt[i].reshape(bp * tp, PE_DIM), w_out, wpe, wpg, pg, final_gain, final)
        xs, ys = _out_embed(act_s, xs, p_sample[i].reshape(bs * ts, PE_DIM), w_out, wpe, wpg, pg, final_gain, final)

    st = lambda name: jnp.stack(outs[name])
    return (yp.reshape(bp, tp, D_MODEL), ys.reshape(bs, ts, D_MODEL),
            st("mc_p"), st("mn_p"), st("mm_p"), st("mc_s"), st("mn_s"), st("mm_s"),
            st("cv_p"), st("cv_s"),
            st("k_p"), st("v_p"), st("f_p"), st("k_s"), st("v_s"), st("f_s"))
```

```python
import functools

import jax
import jax.numpy as jnp
from jax import lax
from jax.experimental import pallas as pl
from jax.experimental.pallas import tpu as pltpu

F32 = jnp.float32
BF16 = jnp.bfloat16

EPS = 1e-6
DEPTH = 4
N_MIXERS = 3
D_MODEL = 1024
D_INNER = 2 * D_MODEL
H_A = 8
DK_A = D_INNER // (2 * H_A)
DV_A = D_INNER // H_A
CONV_W = 3
DH_C = 128
H_C = D_INNER // DH_C
PE_DIM = 256
PAGE_SIZE = 128
LOG2E = 1.4426950408889634

LANES = 128
SUBLANES = 8
CHUNK = 128
VMEM_LIMIT_BYTES = 48 * 1024 * 1024


def _cparams(n_axes):
    return pltpu.CompilerParams(
        dimension_semantics=("arbitrary",) * n_axes,
        vmem_limit_bytes=VMEM_LIMIT_BYTES)


def _log1p(u):
    w = 1.0 + u
    return jnp.where(w == 1.0, u, jnp.log(w) * (u / (w - 1.0)))


def _log_sigmoid(x):
    return jnp.minimum(x, 0.0) - _log1p(jnp.exp(-jnp.abs(x)))


def _silu(x):
    return x * jax.nn.sigmoid(x)


def _rms(x, gain):
    return x * lax.rsqrt(jnp.mean(x * x, axis=-1, keepdims=True) + EPS) * gain


def _split3(x):
    hi = x.astype(BF16)
    r1 = x - hi.astype(F32)
    mid = r1.astype(BF16)
    lo = (r1 - mid.astype(F32)).astype(BF16)
    return hi, mid, lo


def _pad_rows(x, rows):
    if x.shape[0] == rows:
        return x
    return jnp.concatenate([x, jnp.zeros((rows - x.shape[0], x.shape[1]), x.dtype)], axis=0)


PROJ_TN = 2048


def _proj_kernel(x_ref, g_ref, w_ref, wg_ref, cs_ref, o_ref, og_ref, *rest, f32_tiles, scaled):
    extra_refs, h_scr = rest[:-1], rest[-1]
    j = pl.program_id(1)

    @pl.when(j == 0)
    def _():
        h = _rms(x_ref[...], g_ref[...]).astype(BF16)
        h_scr[...] = h
        og_ref[...] = jnp.dot(h, wg_ref[...], preferred_element_type=F32)

    res = jnp.dot(h_scr[...], w_ref[j], preferred_element_type=F32)
    if scaled:
        res = res * cs_ref[j]
    o_ref[...] = res.astype(o_ref.dtype)
    for ref, (lo, hi) in zip(extra_refs, f32_tiles):
        @pl.when((j >= lo) & (j < hi))
        def _(ref=ref):
            ref[...] = res


def _proj(x, gain, w_tiles, wg, out_dtype, col_scale=None, f32_tiles=()):
    m, d = x.shape
    nt, _, tn = w_tiles.shape
    ng = wg.shape[1]
    tm = min(512 if f32_tiles else 1024, m)
    scaled = col_scale is not None
    cs = (col_scale if scaled else jnp.ones((nt * tn,), F32)).reshape(nt, 1, tn)
    resident = lambda shape: pl.BlockSpec(shape, lambda i, j: (0,) * len(shape), pipeline_mode=pl.Buffered(1))
    extra_specs = [pl.BlockSpec((tm, tn), lambda i, j, lo=lo, hi=hi: (i, jnp.clip(j - lo, 0, hi - lo - 1)))
                   for lo, hi in f32_tiles]
    extra_shapes = [jax.ShapeDtypeStruct((m, (hi - lo) * tn), F32) for lo, hi in f32_tiles]
    kern = functools.partial(_proj_kernel, f32_tiles=tuple(f32_tiles), scaled=scaled)
    return pl.pallas_call(
        kern,
        grid=(m // tm, nt),
        in_specs=[
            pl.BlockSpec((tm, d), lambda i, j: (i, 0)),
            pl.BlockSpec((1, d), lambda i, j: (0, 0)),
            resident((nt, d, tn)),
            resident((d, ng)),
            resident((nt, 1, tn)),
        ],
        out_specs=[
            pl.BlockSpec((tm, tn), lambda i, j: (i, j)),
            pl.BlockSpec((tm, ng), lambda i, j: (i, 0)),
        ] + extra_specs,
        out_shape=[
            jax.ShapeDtypeStruct((m, nt * tn), out_dtype),
            jax.ShapeDtypeStruct((m, ng), F32),
        ] + extra_shapes,
        scratch_shapes=[pltpu.VMEM((tm, d), BF16)],
        compiler_params=_cparams(2),
        name="norm_proj",
    )(x, gain, w_tiles, wg, cs)


def _tile_cols(w):
    d, n = w.shape
    return w.reshape(d, n // PROJ_TN, PROJ_TN).transpose(1, 0, 2).astype(BF16)


def _out_embed_kernel(act_ref, x_ref, p_ref, wo_ref, wpe_ref, wpg_ref, g_ref, fg_ref,
                      xo_ref, *maybe_y_ref):
    y = jnp.dot(act_ref[...].astype(BF16), wo_ref[...], preferred_element_type=F32)
    x1 = x_ref[...] + y
    rn = _rms(x1, g_ref[...]).astype(BF16)
    gate = jax.nn.sigmoid(jnp.dot(rn, wpg_ref[...], preferred_element_type=F32))
    pe = jnp.dot(p_ref[...].astype(BF16), wpe_ref[...], preferred_element_type=F32)
    x2 = x1 + pe * gate
    xo_ref[...] = x2
    if maybe_y_ref:
        maybe_y_ref[0][...] = _rms(x2, fg_ref[...])


def _out_embed(act, x, p, wo, wpe, wpg, gain, final_gain, final):
    m, d = x.shape
    di = act.shape[1]
    pe = p.shape[1]
    tm = min(512, m)
    row = lambda i: (i, 0)
    const = lambda i: (0, 0)
    n_out = 2 if final else 1
    outs = pl.pallas_call(
        _out_embed_kernel,
        grid=(m // tm,),
        in_specs=[
            pl.BlockSpec((tm, di), row),
            pl.BlockSpec((tm, d), row),
            pl.BlockSpec((tm, pe), row),
            pl.BlockSpec((di, d), const),
            pl.BlockSpec((pe, d), const),
            pl.BlockSpec((d, d), const),
            pl.BlockSpec((1, d), const),
            pl.BlockSpec((1, d), const),
        ],
        out_specs=[pl.BlockSpec((tm, d), row)] * n_out,
        out_shape=[jax.ShapeDtypeStruct((m, d), F32)] * n_out,
        compiler_params=_cparams(1),
        name="out_embed",
    )(act, x, p, wo, wpe, wpg, gain, final_gain)
    return outs if final else (outs[0], None)


def _mlstm_kernel(q_ref, k_ref, v_ref, o_ref, z_ref, g_ref, bias_ref, hn_ref,
                  c0_ref, n0_ref, m0_ref,
                  act_ref, c_out, n_out, m_out,
                  ct_scr, n_scr, m_scr, *, lv):
    c = pl.program_id(1)
    L = CHUNK
    scale = DK_A ** -0.5

    @pl.when(c == 0)
    def _():
        for h in range(H_A):
            ct_scr[h] = c0_ref[0, h].T
        n_scr[...] = n0_ref[0]
        m_scr[...] = m0_ref[0]

    lq = max(lv, 2 * SUBLANES)
    q = _pad_rows(q_ref[...], lq).astype(BF16)
    k = _pad_rows(k_ref[...], L).astype(BF16)
    v = _pad_rows(v_ref[...], L).astype(BF16)
    g = _pad_rows(g_ref[...], L) + bias_ref[...]
    li = g[:, :LANES]
    lf = _log_sigmoid(g[:, LANES:])
    if lv < L:
        valid = lax.broadcasted_iota(jnp.int32, (L, LANES), 0) < lv
        li = jnp.where(valid, li, -jnp.inf)
        lf = jnp.where(valid, lf, 0.0)

    row = lax.broadcasted_iota(jnp.int32, (L, L), 0)
    col = lax.broadcasted_iota(jnp.int32, (L, L), 1)
    causal = col <= row
    causal_q = causal[:lq]
    tri = jnp.where(causal, 1.0, 0.0).astype(BF16)
    b = sum(jnp.dot(tri, piece, preferred_element_type=F32) for piece in _split3(lf))
    r = li - b
    r_t = r.T
    m_row = m_scr[...]
    b_last = b[L - 1:L, :]
    m_new = jnp.maximum(b_last + m_row, jnp.max(b_last + r, axis=0, keepdims=True))
    decay = jnp.exp(b_last + m_row - m_new)
    ws_t = jnp.exp(b_last + r - m_new).T
    n_all = n_scr[...]
    ones_bf16 = jnp.ones((L, LANES), BF16)

    for h in range(H_A):
        qh = q[:, h * DK_A:(h + 1) * DK_A]
        kh = k[:, h * DK_A:(h + 1) * DK_A]
        vh = v[:, h * DV_A:(h + 1) * DV_A]
        wide = lambda x: jnp.concatenate([x] * (DV_A // LANES), axis=1)
        m0h = m_row[:, h:h + 1]
        rmat = jnp.where(causal_q, r_t[h:h + 1, :], -jnp.inf)
        mx = jnp.broadcast_to(jnp.maximum(jnp.max(rmat, axis=1, keepdims=True), m0h), (lq, LANES))
        w_intra = jnp.exp(rmat - mx)
        w_inter = jnp.exp(m0h - mx) * scale
        qk = lax.dot_general(qh, kh, (((1,), (1,)), ((), ())), preferred_element_type=F32)
        s = qk * scale * w_intra
        s_hi = s.astype(BF16)
        s_lo = (s - s_hi.astype(F32)).astype(BF16)
        ct = ct_scr[h]
        num = (jnp.dot(s_hi, vh, preferred_element_type=F32)
               + wide(w_inter) * jnp.dot(qh, ct.astype(BF16), preferred_element_type=F32))
        n_rows = jnp.broadcast_to(n_all[h:h + 1, :], (LANES, DK_A)).astype(BF16)
        qn = lax.dot_general(qh, n_rows, (((1,), (1,)), ((), ())), preferred_element_type=F32)
        row_sum = (jnp.dot(s_hi, ones_bf16, preferred_element_type=F32)
                   + jnp.dot(s_lo, ones_bf16, preferred_element_type=F32))
        den = row_sum + w_inter * qn
        floor = jnp.exp(-(b[:lq, h:h + 1] + mx))
        hh = num * wide(1.0 / jnp.maximum(jnp.abs(den), floor))
        msq = jnp.broadcast_to(jnp.mean(hh * hh, axis=1, keepdims=True), (lq, LANES))
        hh = hh * wide(lax.rsqrt(msq + EPS))
        hh = hh * hn_ref[:, h * DV_A:(h + 1) * DV_A]
        oh = o_ref[:, h * DV_A:(h + 1) * DV_A].astype(F32)
        zh = z_ref[:, h * DV_A:(h + 1) * DV_A].astype(F32)
        y = hh[:lv] * zh * (1.0 / ((1.0 + jnp.exp(-oh)) * (1.0 + jnp.exp(-zh))))
        act_ref[:, h * DV_A:(h + 1) * DV_A] = y.astype(act_ref.dtype)

        ws_row = ws_t[h:h + 1, :]
        kts = (kh.astype(F32).T * ws_row).astype(BF16)
        dh = decay[:, h:h + 1]
        ct_scr[h] = dh * ct + jnp.dot(kts, vh, preferred_element_type=F32)
        ws8 = jnp.broadcast_to(ws_row, (SUBLANES, L)).astype(BF16)
        n_scr[h:h + 1, :] = dh * n_all[h:h + 1, :] + jnp.dot(ws8, kh, preferred_element_type=F32)[0:1]
    m_scr[...] = m_new

    @pl.when(c == pl.num_programs(1) - 1)
    def _():
        for h in range(H_A):
            c_out[0, h] = ct_scr[h].T
        n_out[0] = n_scr[...]
        m_out[0] = m_scr[...]


def _mlstm(proj, gates, bias, hnorm, c0, n0, m0, bsz, t, act_dtype):
    lv = min(CHUNK, t)
    nc = t // lv
    m = bsz * t
    rows = lambda b, c: b * nc + c
    kern = functools.partial(_mlstm_kernel, lv=lv)
    return pl.pallas_call(
        kern,
        grid=(bsz, nc),
        in_specs=[
            pl.BlockSpec((lv, H_A * DK_A), lambda b, c: (rows(b, c), 0)),
            pl.BlockSpec((lv, H_A * DK_A), lambda b, c: (rows(b, c), 1)),
            pl.BlockSpec((lv, D_INNER), lambda b, c: (rows(b, c), 1)),
            pl.BlockSpec((lv, D_INNER), lambda b, c: (rows(b, c), 2)),
            pl.BlockSpec((lv, D_INNER), lambda b, c: (rows(b, c), 3)),
            pl.BlockSpec((lv, 2 * LANES), lambda b, c: (rows(b, c), 0)),
            pl.BlockSpec((1, 2 * LANES), lambda b, c: (0, 0)),
            pl.BlockSpec((1, D_INNER), lambda b, c: (0, 0)),
            pl.BlockSpec((1, H_A, DV_A, DK_A), lambda b, c: (b, 0, 0, 0)),
            pl.BlockSpec((1, H_A, DK_A), lambda b, c: (b, 0, 0)),
            pl.BlockSpec((1, 1, LANES), lambda b, c: (b, 0, 0)),
        ],
        out_specs=[
            pl.BlockSpec((lv, D_INNER), lambda b, c: (rows(b, c), 0)),
            pl.BlockSpec((1, H_A, DV_A, DK_A), lambda b, c: (b, 0, 0, 0)),
            pl.BlockSpec((1, H_A, DK_A), lambda b, c: (b, 0, 0)),
            pl.BlockSpec((1, 1, LANES), lambda b, c: (b, 0, 0)),
        ],
        out_shape=[
            jax.ShapeDtypeStruct((m, D_INNER), act_dtype),
            jax.ShapeDtypeStruct((bsz, H_A, DV_A, DK_A), F32),
            jax.ShapeDtypeStruct((bsz, H_A, DK_A), F32),
            jax.ShapeDtypeStruct((bsz, 1, LANES), F32),
        ],
        scratch_shapes=[
            pltpu.VMEM((H_A, DK_A, DV_A), F32),
            pltpu.VMEM((H_A, DK_A), F32),
            pltpu.VMEM((1, LANES), F32),
        ],
        compiler_params=_cparams(2),
        name="mlstm_mixer",
    )(proj, proj, proj, proj, proj, gates, bias, hnorm, c0, n0, m0)


def _gated_conv(bg, cg, xin, z, cw, prev):
    u = cg * xin
    p2 = prev[0:1, :]
    p1 = prev[1:2, :]
    row = lax.broadcasted_iota(jnp.int32, u.shape, 0)
    u1 = jnp.where(row == 0, p1, pltpu.roll(u, 1, axis=0))
    u2 = jnp.where(row == 0, p2, jnp.where(row == 1, p1, pltpu.roll(u, 2, axis=0)))
    conv = cw[0:1, :] * u2 + cw[1:2, :] * u1 + cw[2:3, :] * u
    return bg * conv * _silu(z), u[u.shape[0] - (CONV_W - 1):, :]


def _conv_kernel(bg_ref, cg_ref, xin_ref, z_ref, cw_ref, st_ref, act_ref, st_out, carry):
    @pl.when(pl.program_id(1) == 0)
    def _():
        carry[...] = st_ref[0]

    y, last = _gated_conv(bg_ref[...].astype(F32), cg_ref[...].astype(F32), xin_ref[...].astype(F32),
                          z_ref[...].astype(F32), cw_ref[...], carry[...])
    act_ref[...] = y.astype(act_ref.dtype)
    carry[...] = last
    st_out[0] = last


def _conv(proj, conv_w, state, bsz, t, act_dtype):
    tt = min(256, t)
    nt = t // tt
    m = bsz * t
    blk = lambda j: pl.BlockSpec((tt, D_INNER), lambda b, i: (b * nt + i, j))
    return pl.pallas_call(
        _conv_kernel,
        grid=(bsz, nt),
        in_specs=[
            blk(0), blk(1), blk(2), blk(3),
            pl.BlockSpec((CONV_W, D_INNER), lambda b, i: (0, 0)),
            pl.BlockSpec((1, CONV_W - 1, D_INNER), lambda b, i: (b, 0, 0)),
        ],
        out_specs=[
            pl.BlockSpec((tt, D_INNER), lambda b, i: (b * nt + i, 0)),
            pl.BlockSpec((1, CONV_W - 1, D_INNER), lambda b, i: (b, 0, 0)),
        ],
        out_shape=[
            jax.ShapeDtypeStruct((m, D_INNER), act_dtype),
            jax.ShapeDtypeStruct((bsz, CONV_W - 1, D_INNER), F32),
        ],
        scratch_shapes=[pltpu.VMEM((CONV_W - 1, D_INNER), F32)],
        compiler_params=_cparams(2),
        name="conv_mixer",
    )(proj, proj, proj, proj, conv_w, state)


def _proj_conv_kernel(x_ref, g_ref, w_ref, cw_ref, st_ref, act_ref, st_out, h_scr, carry, *, tiles_per_seq):
    i = pl.program_id(0)
    j = pl.program_id(1)
    cs = act_ref.shape[1]

    @pl.when(j == 0)
    def _():
        h_scr[...] = _rms(x_ref[...], g_ref[...]).astype(BF16)

    res = jnp.dot(h_scr[...], w_ref[j], preferred_element_type=F32)
    prev = jnp.where(i % tiles_per_seq == 0, st_ref[0], carry[j])
    y, last = _gated_conv(res[:, :cs], res[:, cs:2 * cs], res[:, 2 * cs:3 * cs], res[:, 3 * cs:],
                          cw_ref[j], prev)
    act_ref[...] = y.astype(act_ref.dtype)
    carry[j] = last
    st_out[0] = last


def _proj_conv(x, gain, w, conv_w, state, bsz, t):
    m, d = x.shape
    tm = 512
    assert t % tm == 0
    cs = PROJ_TN // 4
    nt = D_INNER // cs
    w_tiles = w.reshape(d, 4, nt, cs).transpose(2, 0, 1, 3).reshape(nt, d, 4 * cs).astype(BF16)
    cw_tiles = conv_w.reshape(CONV_W, nt, cs).transpose(1, 0, 2)
    tiles_per_seq = t // tm
    resident = lambda shape: pl.BlockSpec(shape, lambda i, j: (0,) * len(shape), pipeline_mode=pl.Buffered(1))
    state_spec = pl.BlockSpec((1, CONV_W - 1, cs), lambda i, j: (i // tiles_per_seq, 0, j))
    tile_state_spec = pl.BlockSpec((1, CONV_W - 1, cs), lambda i, j: (i, 0, j))
    act, tile_states = pl.pallas_call(
        functools.partial(_proj_conv_kernel, tiles_per_seq=tiles_per_seq),
        grid=(m // tm, nt),
        in_specs=[
            pl.BlockSpec((tm, d), lambda i, j: (i, 0)),
            pl.BlockSpec((1, d), lambda i, j: (0, 0)),
            resident((nt, d, 4 * cs)),
            resident((nt, CONV_W, cs)),
            state_spec,
        ],
        out_specs=[pl.BlockSpec((tm, cs), lambda i, j: (i, j)), tile_state_spec],
        out_shape=[
            jax.ShapeDtypeStruct((m, D_INNER), BF16),
            jax.ShapeDtypeStruct((m // tm, CONV_W - 1, D_INNER), F32),
        ],
        scratch_shapes=[pltpu.VMEM((tm, d), BF16), pltpu.VMEM((nt, CONV_W - 1, cs), F32)],
        compiler_params=_cparams(2),
        name="norm_proj_conv",
    )(x, gain, w_tiles, cw_tiles, state)
    return act, tile_states[tiles_per_seq - 1::tiles_per_seq]


AUX_K0 = 3 * H_C


def _fgate_kernel(g_ref, bias_ref, lf_ref, c_ref, *rest, lv, with_aux):
    carry = rest[-1]
    L = CHUNK

    @pl.when(pl.program_id(1) == 0)
    def _():
        carry[...] = jnp.zeros_like(carry)

    lf = _log_sigmoid(g_ref[...] + bias_ref[...])
    lf_ref[...] = lf
    lf = _pad_rows(lf, L)
    if lv < L:
        lf = jnp.where(lax.broadcasted_iota(jnp.int32, (L, LANES), 0) < lv, lf, 0.0)
    row = lax.broadcasted_iota(jnp.int32, (L, L), 0)
    col = lax.broadcasted_iota(jnp.int32, (L, L), 1)
    tri = jnp.where(col <= row, 1.0, 0.0).astype(BF16)
    cs = sum(jnp.dot(tri, piece, preferred_element_type=F32) for piece in _split3(lf)) + carry[...]
    c_ref[...] = cs[:lv]
    carry[...] = cs[L - 1:L, :]
    if with_aux:
        qa_ref, ka_ref = rest[0], rest[1]
        qa = jnp.zeros((L, LANES), F32)
        ka = jnp.zeros((L, LANES), F32)
        head_ok = row < H_C
        for j, piece in enumerate(_split3(cs * LOG2E)):
            to_q = jnp.where(head_ok & (col == 3 * row + j), 1.0, 0.0).astype(BF16)
            to_k = jnp.where(head_ok & (col == AUX_K0 + 3 * row + j), 1.0, 0.0).astype(BF16)
            qa = qa + jnp.dot(piece, to_q, preferred_element_type=F32)
            ka = ka - jnp.dot(piece, to_k, preferred_element_type=F32)
        qa_ref[...] = qa.astype(BF16)
        ka_ref[...] = ka.astype(BF16)


def _fgate(gates, bias, bsz, t, with_aux):
    lv = min(CHUNK, t)
    nc = t // lv
    m = bsz * t
    rows = pl.BlockSpec((lv, LANES), lambda b, c: (b * nc + c, 0))
    aux_spec = rows
    aux_shape = jax.ShapeDtypeStruct((m, LANES), BF16)
    kern = functools.partial(_fgate_kernel, lv=lv, with_aux=with_aux)
    return pl.pallas_call(
        kern,
        grid=(bsz, nc),
        in_specs=[rows, pl.BlockSpec((1, LANES), lambda b, c: (0, 0))],
        out_specs=[rows, rows] + [aux_spec] * (2 * with_aux),
        out_shape=[jax.ShapeDtypeStruct((m, LANES), F32)] * 2 + [aux_shape] * (2 * with_aux),
        scratch_shapes=[pltpu.VMEM((1, LANES), F32)],
        compiler_params=_cparams(2),
        name="fgate_scan",
    )(gates, bias)


def _online_softmax_step(s_blocks, m, l, acc, values):
    bmax = functools.reduce(jnp.maximum, s_blocks)
    m_new = jnp.maximum(m, jnp.max(bmax, axis=1, keepdims=True))
    corr = jnp.exp2(m - m_new)
    ps = [jnp.exp2(blk - m_new) for blk in s_blocks]
    l_new = l * corr + functools.reduce(jnp.add, ps)
    p = ps[0] if len(ps) == 1 else jnp.concatenate(ps, axis=1)
    acc_new = acc * corr + jnp.dot(p.astype(BF16), values, preferred_element_type=F32)
    return m_new, l_new, acc_new


def _lane_blocks(s):
    return [s[:, b * LANES:(b + 1) * LANES] for b in range(s.shape[1] // LANES)]


FLASH_T = 256


def _flash_kernel(q_ref, k_ref, v_ref, z_ref, qa_ref, ka_ref, o_ref, *, tq):
    nq = q_ref.shape[0] // tq
    causal = (lax.broadcasted_iota(jnp.int32, (tq, tq), 1)
              <= lax.broadcasted_iota(jnp.int32, (tq, tq), 0))
    lane = lax.broadcasted_iota(jnp.int32, (1, LANES), 1) - 3 * pl.program_id(1)
    ones_q = jnp.where((lane >= AUX_K0) & (lane < AUX_K0 + 3), 1.0, 0.0).astype(BF16)
    ones_k = jnp.where((lane >= 0) & (lane < 3), 1.0, 0.0).astype(BF16)
    for i in range(nq):
        rs = slice(i * tq, (i + 1) * tq)
        q_aug = jnp.concatenate([q_ref[rs, :], qa_ref[rs, :] + ones_q], axis=1)
        m = jnp.full((tq, LANES), -jnp.inf, F32)
        l = jnp.zeros((tq, LANES), F32)
        acc = jnp.zeros((tq, DH_C), F32)
        for j in range(i + 1):
            ks = slice(j * tq, (j + 1) * tq)
            k_aug = jnp.concatenate([k_ref[ks, :], ka_ref[ks, :] + ones_k], axis=1)
            s = lax.dot_general(q_aug, k_aug, (((1,), (1,)), ((), ())), preferred_element_type=F32)
            if j == i:
                s = jnp.where(causal, s, -jnp.inf)
            m, l, acc = _online_softmax_step(_lane_blocks(s), m, l, acc, v_ref[ks, :])
        o = acc * (1.0 / jnp.sum(l, axis=1, keepdims=True))
        o_ref[rs, :] = (o * _silu(z_ref[rs, :].astype(F32))).astype(o_ref.dtype)


def _flash(proj, q_aux, k_aux, bsz, t):
    tq = min(FLASH_T, t)
    col = lambda j: pl.BlockSpec((t, DH_C), lambda b, h: (b, j * H_C + h))
    aux = pl.BlockSpec((t, LANES), lambda b, h: (b, 0))
    return pl.pallas_call(
        functools.partial(_flash_kernel, tq=tq),
        grid=(bsz, H_C),
        in_specs=[col(0), col(1), col(2), col(3), aux, aux],
        out_specs=pl.BlockSpec((t, DH_C), lambda b, h: (b, h)),
        out_shape=jax.ShapeDtypeStruct((bsz * t, D_INNER), BF16),
        compiler_params=_cparams(2),
        name="fox_flash",
    )(proj, proj, proj, proj, q_aux, k_aux)


PAGES_PER_STEP = 8


def _paged_kernel(pt_ref, q_ref, kn_ref, vn_ref, z_ref, cqc_ref, cnf_ref, *rest, t):
    page_refs = rest[:3 * PAGES_PER_STEP]
    out_ref, qs_scr, wq_scr, mask_scr, spread_scr, m_scr, l_scr, acc_scr, carry = rest[3 * PAGES_PER_STEP:]
    p = pl.program_id(1)
    P = PAGE_SIZE
    R = H_C * t
    W = P * H_C
    t_shift = t.bit_length() - 1
    h_shift = H_C.bit_length() - 1

    @pl.when(p == 0)
    def _():
        m_scr[...] = jnp.full_like(m_scr, -jnp.inf)
        l_scr[...] = jnp.zeros_like(l_scr)
        acc_scr[...] = jnp.zeros_like(acc_scr)
        carry[...] = jnp.zeros_like(carry)
        q = q_ref[...] * (DH_C ** -0.5 * LOG2E)
        q_rows = jnp.concatenate([q[:, h * DH_C:(h + 1) * DH_C] for h in range(H_C)], axis=0)
        qs_scr[...] = q_rows.astype(BF16)
        q_t = q_rows.T.astype(BF16)
        zero = jnp.zeros_like(q_t)
        wq_scr[...] = jnp.concatenate([jnp.concatenate([q_t, zero], axis=1),
                                       jnp.concatenate([zero, q_t], axis=1)], axis=0)
        row_h = lax.shift_right_logical(lax.broadcasted_iota(jnp.int32, (R, W), 0), t_shift)
        col_h = lax.broadcasted_iota(jnp.int32, (R, W), 1) & (H_C - 1)
        mask_scr[...] = jnp.where(row_h == col_h, cqc_ref[0] * LOG2E, -jnp.inf)
        pos = lax.broadcasted_iota(jnp.int32, (P, W), 0)
        col_p = lax.shift_right_logical(lax.broadcasted_iota(jnp.int32, (P, W), 1), h_shift)
        spread_scr[...] = jnp.where(pos > col_p, 1.0, 0.0).astype(BF16)

    qs = qs_scr[...]
    head = lax.broadcasted_iota(jnp.int32, (H_C, W), 0)
    col_h16 = lax.broadcasted_iota(jnp.int32, (H_C, W), 1) & (H_C - 1)
    state = (m_scr[...], l_scr[...], acc_scr[...])
    suffix_carry = carry[...]
    pages = list(zip(*[page_refs[i::3] for i in range(3)]))
    pieces = jnp.concatenate([piece for _, _, lfp_ref in pages for piece in _split3(lfp_ref[0])], axis=0)
    g_all = jnp.dot(pieces, spread_scr[...], preferred_element_type=F32)
    half = W // 2
    for i, (kp_ref, vp_ref, lfp_ref) in enumerate(pages):
        g = sum(g_all[(3 * i + j) * H_C:(3 * i + j + 1) * H_C, :] for j in range(3)) + suffix_carry
        bias = jnp.sum(jnp.where(head == col_h16, g, 0.0), axis=0, keepdims=True) * LOG2E
        suffix_carry = suffix_carry + jnp.sum(lfp_ref[0], axis=1, keepdims=True)
        keys = jnp.concatenate([kp_ref[0, :half, :].astype(BF16), kp_ref[0, half:, :].astype(BF16)], axis=1)
        st = jnp.dot(keys, wq_scr[...], preferred_element_type=F32)
        s = jnp.concatenate([st[:, :R].T, st[:, R:].T], axis=1)
        state = _online_softmax_step(_lane_blocks(s + mask_scr[...] + bias), *state, vp_ref[0].astype(BF16))
    carry[...] = suffix_carry
    m_scr[...], l_scr[...], acc_scr[...] = state

    @pl.when(p == pl.num_programs(1) - 1)
    def _():
        sn = lax.dot_general(qs, kn_ref[0].astype(BF16), (((1,), (1,)), ((), ())), preferred_element_type=F32)
        row = lax.broadcasted_iota(jnp.int32, (R, R), 0)
        col = lax.broadcasted_iota(jnp.int32, (R, R), 1)
        valid = (((col & (H_C - 1)) == lax.shift_right_logical(row, t_shift))
                 & (lax.shift_right_logical(col, h_shift) <= (row & (t - 1))))
        sn = jnp.where(valid, sn + (cqc_ref[0] - cnf_ref[0]) * LOG2E, -jnp.inf)
        _, l, acc = _online_softmax_step([sn], m_scr[...], l_scr[...], acc_scr[...], vn_ref[0].astype(BF16))
        o = acc * (1.0 / jnp.sum(l, axis=1, keepdims=True))
        for h in range(H_C):
            zh = z_ref[:, h * DH_C:(h + 1) * DH_C]
            out_ref[:, h * DH_C:(h + 1) * DH_C] = (o[h * t:(h + 1) * t, :] * _silu(zh)).astype(out_ref.dtype)


def _paged(page_table, proj, kn_rows, vn_rows, cq_col, cn_flat, pool_k, pool_v, pool_lf_t, bsz, t):
    assert t & (t - 1) == 0 and t * H_C == LANES, "sample length must be a power of two with t*H == 128"
    n_pages = page_table.shape[1]
    g = PAGES_PER_STEP
    assert n_pages % g == 0
    R = H_C * t
    W = PAGE_SIZE * H_C
    per_b = lambda b, p, pt: (b, 0, 0)
    page_specs = []
    for i in range(g):
        page = lambda b, p, pt, i=i: (pt[b, n_pages - 1 - (p * g + i)], 0, 0)
        page_specs += [pl.BlockSpec((1, W, DH_C), page), pl.BlockSpec((1, W, DH_C), page),
                       pl.BlockSpec((1, H_C, PAGE_SIZE), page)]
    grid_spec = pltpu.PrefetchScalarGridSpec(
        num_scalar_prefetch=1,
        grid=(bsz, n_pages // g),
        in_specs=[
            pl.BlockSpec((t, D_INNER), lambda b, p, pt: (b, 0)),
            pl.BlockSpec((1, R, DH_C), per_b),
            pl.BlockSpec((1, R, DH_C), per_b),
            pl.BlockSpec((t, D_INNER), lambda b, p, pt: (b, 3)),
            pl.BlockSpec((1, R, 1), per_b),
            pl.BlockSpec((1, 1, R), per_b),
        ] + page_specs,
        out_specs=pl.BlockSpec((t, D_INNER), lambda b, p, pt: (b, 0)),
        scratch_shapes=[
            pltpu.VMEM((R, DH_C), BF16),
            pltpu.VMEM((2 * DH_C, 2 * R), BF16),
            pltpu.VMEM((R, W), F32),
            pltpu.VMEM((PAGE_SIZE, W), BF16),
            pltpu.VMEM((R, LANES), F32),
            pltpu.VMEM((R, LANES), F32),
            pltpu.VMEM((R, DH_C), F32),
            pltpu.VMEM((H_C, 1), F32),
        ],
    )
    return pl.pallas_call(
        functools.partial(_paged_kernel, t=t),
        grid_spec=grid_spec,
        out_shape=jax.ShapeDtypeStruct((bsz * t, D_INNER), F32),
        compiler_params=_cparams(2),
        name="fox_paged",
    )(page_table, proj, kn_rows, vn_rows, proj, cq_col, cn_flat, *([pool_k, pool_v, pool_lf_t] * g))


def _pad_cols(w, n):
    return jnp.pad(w, ((0, 0), (0, n - w.shape[1])))


def _lane_row(vec, offset=0, width=LANES):
    return jnp.zeros((1, width), F32).at[0, offset:offset + vec.shape[0]].set(vec)


def kernel(x_prompt, x_sample, state_mlstm_c, state_mlstm_n, state_mlstm_m, state_conv, cache_k, cache_v, cache_logf, page_table, p_prompt, p_sample, ln_gain, w_in_a, b_igate_a, b_fgate_a, hnorm_a, w_out_a, w_in_b, conv_w_b, w_out_b, w_in_c, b_fgate_c, w_out_c, w_pe, w_pg, pe_norm, final_norm):
    bp, tp, _ = x_prompt.shape
    bs, ts, _ = x_sample.shape
    xp = x_prompt.reshape(bp * tp, D_MODEL)
    xs = x_sample.reshape(bs * ts, D_MODEL)
    final_gain = final_norm.reshape(1, D_MODEL)
    qk = H_A * DK_A
    g0 = 2 * qk + D_INNER
    f0 = 3 * D_INNER

    outs = {name: [] for name in (
        "mc_p", "mn_p", "mm_p", "mc_s", "mn_s", "mm_s", "cv_p", "cv_s",
        "k_p", "v_p", "f_p", "k_s", "v_s", "f_s")}
    yp = ys = None
    for i in range(DEPTH):
        j = i // N_MIXERS
        gain = ln_gain[i].reshape(1, D_MODEL)
        kind = i % N_MIXERS
        if kind == 0:
            w = w_in_a[j]
            w_main = _tile_cols(jnp.concatenate([w[:, :g0], w[:, g0 + 2 * H_A:]], axis=1))
            w_gate = jnp.concatenate([_pad_cols(w[:, g0:g0 + H_A], LANES),
                                      _pad_cols(w[:, g0 + H_A:g0 + 2 * H_A], LANES)], axis=1).astype(BF16)
            bias = jnp.concatenate([_lane_row(b_igate_a[j]), _lane_row(b_fgate_a[j])], axis=1)
            hn = hnorm_a[j].reshape(1, D_INNER)
            w_out = w_out_a[j]

            proj_p, gates_p = _proj(xp, gain, w_main, w_gate, BF16)
            act_p, c_p, n_p, m_p = _mlstm(
                proj_p, gates_p, bias, hn,
                jnp.zeros((bp, H_A, DV_A, DK_A), F32), jnp.zeros((bp, H_A, DK_A), F32),
                jnp.zeros((bp, 1, LANES), F32), bp, tp, BF16)
            proj_s, gates_s = _proj(xs, gain, w_main, w_gate, F32)
            m0 = jnp.pad(state_mlstm_m[j], ((0, 0), (0, LANES - H_A))).reshape(bs, 1, LANES)
            act_s, c_s, n_s, m_s = _mlstm(
                proj_s, gates_s, bias, hn, state_mlstm_c[j], state_mlstm_n[j], m0, bs, ts, F32)
            outs["mc_p"].append(c_p); outs["mn_p"].append(n_p); outs["mm_p"].append(m_p[:, 0, :H_A])
            outs["mc_s"].append(c_s); outs["mn_s"].append(n_s); outs["mm_s"].append(m_s[:, 0, :H_A])
        elif kind == 1:
            w_main = _tile_cols(w_in_b[j])
            w_gate = jnp.zeros((D_MODEL, LANES), BF16)
            w_out = w_out_b[j]
            act_p, st_p = _proj_conv(xp, gain, w_in_b[j], conv_w_b[j],
                                     jnp.zeros((bp, CONV_W - 1, D_INNER), F32), bp, tp)
            proj_s, _ = _proj(xs, gain, w_main, w_gate, F32)
            act_s, st_s = _conv(proj_s, conv_w_b[j], state_conv[j], bs, ts, F32)
            outs["cv_p"].append(st_p); outs["cv_s"].append(st_s)
        else:
            w = w_in_c[j]
            w_main = _tile_cols(jnp.concatenate([w[:, :f0], w[:, f0 + H_C:]], axis=1))
            w_gate = _pad_cols(w[:, f0:f0 + H_C], LANES).astype(BF16)
            bias = _lane_row(b_fgate_c[j])
            w_out = w_out_c[j]

            q_scale = jnp.ones((4 * D_INNER,), F32).at[:D_INNER].set(DH_C ** -0.5 * LOG2E)
            kt = D_INNER // PROJ_TN
            proj_p, gates_p, k_p, v_p = _proj(xp, gain, w_main, w_gate, BF16, col_scale=q_scale,
                                              f32_tiles=((kt, 2 * kt), (2 * kt, 3 * kt)))
            lf_p, _, q_aux, k_aux = _fgate(gates_p, bias, bp, tp, True)
            act_p = _flash(proj_p, q_aux, k_aux, bp, tp)
            outs["k_p"].append(k_p.reshape(bp, tp, H_C, DH_C))
            outs["v_p"].append(v_p.reshape(bp, tp, H_C, DH_C))
            outs["f_p"].append(lf_p[:, :H_C].reshape(bp, tp, H_C))

            proj_s, gates_s = _proj(xs, gain, w_main, w_gate, F32)
            lf_s, c_s = _fgate(gates_s, bias, bs, ts, False)
            k_s = proj_s[:, D_INNER:2 * D_INNER]
            v_s = proj_s[:, 2 * D_INNER:3 * D_INNER]
            c_new = c_s.reshape(bs, ts, LANES)[:, :, :H_C]
            cq_col = c_new.transpose(0, 2, 1).reshape(bs, H_C * ts, 1)
            cn_flat = c_new.reshape(bs, 1, ts * H_C)
            n_pool = cache_k.shape[1]
            pool_k = cache_k[j].reshape(n_pool, PAGE_SIZE * H_C, DH_C)
            pool_v = cache_v[j].reshape(n_pool, PAGE_SIZE * H_C, DH_C)
            pool_lf_t = cache_logf[j].transpose(0, 2, 1)
            act_s = _paged(page_table, proj_s, k_s.reshape(bs, ts * H_C, DH_C), v_s.reshape(bs, ts * H_C, DH_C),
                           cq_col, cn_flat, pool_k, pool_v, pool_lf_t, bs, ts)
            outs["k_s"].append(k_s.reshape(bs, ts, H_C, DH_C))
            outs["v_s"].append(v_s.reshape(bs, ts, H_C, DH_C))
            outs["f_s"].append(lf_s[:, :H_C].reshape(bs, ts, H_C))

        final = i == DEPTH - 1
        w_out = w_out.astype(BF16)
        wpe = w_pe[i].astype(BF16)
        wpg = w_pg[i].astype(BF16)
        pg = pe_norm[i].reshape(1, D_MODEL)
        xp, yp = _out_embed(act_p, xp, p_prompt[i].reshape(bp * tp, PE_DIM), w_out, wpe, wpg, pg, final_gain, final)
        xs, ys = _out_embed(act_s, xs, p_sample[i].reshape(bs * ts, PE_DIM), w_out, wpe, wpg, pg, final_gain, final)

    st = lambda name: jnp.stack(outs[name]) if len(outs[name]) > 1 else outs[name][0][None]
    return (yp.reshape(bp, tp, D_MODEL), ys.reshape(bs, ts, D_MODEL),
            st("mc_p"), st("mn_p"), st("mm_p"), st("mc_s"), st("mn_s"), st("mm_s"),
            st("cv_p"), st("cv_s"),
            st("k_p"), st("v_p"), st("f_p"), st("k_s"), st("v_s"), st("f_s"))
```

```python
import functools

import jax
import jax.numpy as jnp
from jax import lax
from jax.experimental import pallas as pl
from jax.experimental.pallas import tpu as pltpu

F32 = jnp.float32
BF16 = jnp.bfloat16

EPS = 1e-6
DEPTH = 4
N_MIXERS = 3
D_MODEL = 1024
D_INNER = 2 * D_MODEL
H_A = 8
DK_A = D_INNER // (2 * H_A)
DV_A = D_INNER // H_A
CONV_W = 3
DH_C = 128
H_C = D_INNER // DH_C
PE_DIM = 256
PAGE_SIZE = 128
LOG2E = 1.4426950408889634

LANES = 128
SUBLANES = 8
CHUNK = 128
VMEM_LIMIT_BYTES = 48 * 1024 * 1024


def _cparams(n_axes):
    return pltpu.CompilerParams(
        dimension_semantics=("arbitrary",) * n_axes,
        vmem_limit_bytes=VMEM_LIMIT_BYTES)


def _log1p(u):
    w = 1.0 + u
    return jnp.where(w == 1.0, u, jnp.log(w) * (u / (w - 1.0)))


def _log_sigmoid(x):
    return jnp.minimum(x, 0.0) - _log1p(jnp.exp(-jnp.abs(x)))


def _silu(x):
    return x * jax.nn.sigmoid(x)


def _rms(x, gain):
    return x * lax.rsqrt(jnp.mean(x * x, axis=-1, keepdims=True) + EPS) * gain


def _split3(x):
    hi = x.astype(BF16)
    r1 = x - hi.astype(F32)
    mid = r1.astype(BF16)
    lo = (r1 - mid.astype(F32)).astype(BF16)
    return hi, mid, lo


def _pad_rows(x, rows):
    if x.shape[0] == rows:
        return x
    return jnp.concatenate([x, jnp.zeros((rows - x.shape[0], x.shape[1]), x.dtype)], axis=0)


PROJ_TN = 2048


def _proj_kernel(x_ref, g_ref, w_ref, wg_ref, cs_ref, o_ref, og_ref, *rest, f32_tiles, scaled, tn):
    extra_refs, h_scr = rest[:-1], rest[-1]
    j = pl.program_id(1)

    @pl.when(j == 0)
    def _():
        h = _rms(x_ref[...], g_ref[...]).astype(BF16)
        h_scr[...] = h
        og_ref[...] = jnp.dot(h, wg_ref[...], preferred_element_type=F32)

    res = jnp.dot(h_scr[...], w_ref[:, pl.ds(pl.multiple_of(j * tn, tn), tn)], preferred_element_type=F32)
    if scaled:
        res = res * cs_ref[j]
    o_ref[...] = res.astype(o_ref.dtype)
    for ref, (lo, hi) in zip(extra_refs, f32_tiles):
        @pl.when((j >= lo) & (j < hi))
        def _(ref=ref):
            ref[...] = res


def _proj(x, gain, w_tiles, wg, out_dtype, col_scale=None, f32_tiles=()):
    m, d = x.shape
    tn = PROJ_TN
    nt = w_tiles.shape[1] // tn
    ng = wg.shape[1]
    tm = min(512 if f32_tiles else 1024, m)
    scaled = col_scale is not None
    cs = (col_scale if scaled else jnp.ones((nt * tn,), F32)).reshape(nt, 1, tn)
    resident = lambda shape: pl.BlockSpec(shape, lambda i, j: (0,) * len(shape), pipeline_mode=pl.Buffered(1))
    extra_specs = [pl.BlockSpec((tm, tn), lambda i, j, lo=lo, hi=hi: (i, jnp.clip(j - lo, 0, hi - lo - 1)))
                   for lo, hi in f32_tiles]
    extra_shapes = [jax.ShapeDtypeStruct((m, (hi - lo) * tn), F32) for lo, hi in f32_tiles]
    kern = functools.partial(_proj_kernel, f32_tiles=tuple(f32_tiles), scaled=scaled, tn=tn)
    return pl.pallas_call(
        kern,
        grid=(m // tm, nt),
        in_specs=[
            pl.BlockSpec((tm, d), lambda i, j: (i, 0)),
            pl.BlockSpec((1, d), lambda i, j: (0, 0)),
            resident((d, nt * tn)),
            resident((d, ng)),
            resident((nt, 1, tn)),
        ],
        out_specs=[
            pl.BlockSpec((tm, tn), lambda i, j: (i, j)),
            pl.BlockSpec((tm, ng), lambda i, j: (i, 0)),
        ] + extra_specs,
        out_shape=[
            jax.ShapeDtypeStruct((m, nt * tn), out_dtype),
            jax.ShapeDtypeStruct((m, ng), F32),
        ] + extra_shapes,
        scratch_shapes=[pltpu.VMEM((tm, d), BF16)],
        compiler_params=_cparams(2),
        name="norm_proj",
    )(x, gain, w_tiles, wg, cs)


def _tile_cols(w):
    return w.astype(BF16)


def _out_embed_kernel(act_ref, x_ref, p_ref, wo_ref, wpe_ref, wpg_ref, g_ref, fg_ref,
                      xo_ref, *maybe_y_ref):
    y = jnp.dot(act_ref[...].astype(BF16), wo_ref[...], preferred_element_type=F32)
    x1 = x_ref[...] + y
    rn = _rms(x1, g_ref[...]).astype(BF16)
    gate = jax.nn.sigmoid(jnp.dot(rn, wpg_ref[...], preferred_element_type=F32))
    pe = jnp.dot(p_ref[...].astype(BF16), wpe_ref[...], preferred_element_type=F32)
    x2 = x1 + pe * gate
    xo_ref[...] = x2
    if maybe_y_ref:
        maybe_y_ref[0][...] = _rms(x2, fg_ref[...])


def _out_embed(act, x, p, wo, wpe, wpg, gain, final_gain, final):
    m, d = x.shape
    di = act.shape[1]
    pe = p.shape[1]
    tm = min(1024, m)
    row = lambda i: (i, 0)
    const = lambda i: (0, 0)
    resident = lambda shape: pl.BlockSpec(shape, const, pipeline_mode=pl.Buffered(1))
    n_out = 2 if final else 1
    outs = pl.pallas_call(
        _out_embed_kernel,
        grid=(m // tm,),
        in_specs=[
            pl.BlockSpec((tm, di), row),
            pl.BlockSpec((tm, d), row),
            pl.BlockSpec((tm, pe), row),
            resident((di, d)),
            resident((pe, d)),
            resident((d, d)),
            resident((1, d)),
            resident((1, d)),
        ],
        out_specs=[pl.BlockSpec((tm, d), row)] * n_out,
        out_shape=[jax.ShapeDtypeStruct((m, d), F32)] * n_out,
        compiler_params=_cparams(1),
        name="out_embed",
    )(act, x, p, wo, wpe, wpg, gain, final_gain)
    return outs if final else (outs[0], None)


def _mlstm_kernel(q_ref, k_ref, v_ref, o_ref, z_ref, g_ref, bias_ref, hn_ref,
                  c0_ref, n0_ref, m0_ref,
                  act_ref, c_out, n_out, m_out,
                  ct_scr, n_scr, m_scr, *, lv):
    c = pl.program_id(1)
    L = CHUNK
    scale = DK_A ** -0.5

    @pl.when(c == 0)
    def _():
        for h in range(H_A):
            ct_scr[h] = c0_ref[0, h].T
        n_scr[...] = n0_ref[0]
        m_scr[...] = m0_ref[0]

    lq = max(lv, 2 * SUBLANES)
    q = _pad_rows(q_ref[...], lq).astype(BF16)
    k = _pad_rows(k_ref[...], L).astype(BF16)
    v = _pad_rows(v_ref[...], L).astype(BF16)
    g = _pad_rows(g_ref[...], L) + bias_ref[...]
    li = g[:, :LANES]
    lf = _log_sigmoid(g[:, LANES:])
    if lv < L:
        valid = lax.broadcasted_iota(jnp.int32, (L, LANES), 0) < lv
        li = jnp.where(valid, li, -jnp.inf)
        lf = jnp.where(valid, lf, 0.0)

    row = lax.broadcasted_iota(jnp.int32, (L, L), 0)
    col = lax.broadcasted_iota(jnp.int32, (L, L), 1)
    causal = col <= row
    causal_q = causal[:lq]
    tri = jnp.where(causal, 1.0, 0.0).astype(BF16)
    b = sum(jnp.dot(tri, piece, preferred_element_type=F32) for piece in _split3(lf))
    r = li - b
    r_t = r.T
    m_row = m_scr[...]
    b_last = b[L - 1:L, :]
    m_new = jnp.maximum(b_last + m_row, jnp.max(b_last + r, axis=0, keepdims=True))
    decay = jnp.exp(b_last + m_row - m_new)
    ws_t = jnp.exp(b_last + r - m_new).T
    n_all = n_scr[...]
    ones_bf16 = jnp.ones((L, LANES), BF16)

    for h in range(H_A):
        qh = q[:, h * DK_A:(h + 1) * DK_A]
        kh = k[:, h * DK_A:(h + 1) * DK_A]
        vh = v[:, h * DV_A:(h + 1) * DV_A]
        wide = lambda x: jnp.concatenate([x] * (DV_A // LANES), axis=1)
        m0h = m_row[:, h:h + 1]
        rmat = jnp.where(causal_q, r_t[h:h + 1, :], -jnp.inf)
        mx = jnp.broadcast_to(jnp.maximum(jnp.max(rmat, axis=1, keepdims=True), m0h), (lq, LANES))
        w_intra = jnp.exp(rmat - mx)
        w_inter = jnp.exp(m0h - mx) * scale
        qk = lax.dot_general(qh, kh, (((1,), (1,)), ((), ())), preferred_element_type=F32)
        s = qk * scale * w_intra
        s_hi = s.astype(BF16)
        s_lo = (s - s_hi.astype(F32)).astype(BF16)
        ct = ct_scr[h]
        num = (jnp.dot(s_hi, vh, preferred_element_type=F32)
               + wide(w_inter) * jnp.dot(qh, ct.astype(BF16), preferred_element_type=F32))
        n_rows = jnp.broadcast_to(n_all[h:h + 1, :], (LANES, DK_A)).astype(BF16)
        qn = lax.dot_general(qh, n_rows, (((1,), (1,)), ((), ())), preferred_element_type=F32)
        row_sum = (jnp.dot(s_hi, ones_bf16, preferred_element_type=F32)
                   + jnp.dot(s_lo, ones_bf16, preferred_element_type=F32))
        den = row_sum + w_inter * qn
        floor = jnp.exp(-(b[:lq, h:h + 1] + mx))
        hh = num * wide(1.0 / jnp.maximum(jnp.abs(den), floor))
        msq = jnp.broadcast_to(jnp.mean(hh * hh, axis=1, keepdims=True), (lq, LANES))
        hh = hh * wide(lax.rsqrt(msq + EPS))
        hh = hh * hn_ref[:, h * DV_A:(h + 1) * DV_A]
        oh = o_ref[:, h * DV_A:(h + 1) * DV_A].astype(F32)
        zh = z_ref[:, h * DV_A:(h + 1) * DV_A].astype(F32)
        y = hh[:lv] * zh * (1.0 / ((1.0 + jnp.exp(-oh)) * (1.0 + jnp.exp(-zh))))
        act_ref[:, h * DV_A:(h + 1) * DV_A] = y.astype(act_ref.dtype)

        ws_row = ws_t[h:h + 1, :]
        kts = (kh.astype(F32).T * ws_row).astype(BF16)
        dh = decay[:, h:h + 1]
        ct_scr[h] = dh * ct + jnp.dot(kts, vh, preferred_element_type=F32)
        ws8 = jnp.broadcast_to(ws_row, (SUBLANES, L)).astype(BF16)
        n_scr[h:h + 1, :] = dh * n_all[h:h + 1, :] + jnp.dot(ws8, kh, preferred_element_type=F32)[0:1]
    m_scr[...] = m_new

    @pl.when(c == pl.num_programs(1) - 1)
    def _():
        for h in range(H_A):
            c_out[0, h] = ct_scr[h].T
        n_out[0] = n_scr[...]
        m_out[0] = m_scr[...]


def _mlstm(proj, gates, bias, hnorm, c0, n0, m0, bsz, t, act_dtype):
    lv = min(CHUNK, t)
    nc = t // lv
    m = bsz * t
    rows = lambda b, c: b * nc + c
    kern = functools.partial(_mlstm_kernel, lv=lv)
    return pl.pallas_call(
        kern,
        grid=(bsz, nc),
        in_specs=[
            pl.BlockSpec((lv, H_A * DK_A), lambda b, c: (rows(b, c), 0)),
            pl.BlockSpec((lv, H_A * DK_A), lambda b, c: (rows(b, c), 1)),
            pl.BlockSpec((lv, D_INNER), lambda b, c: (rows(b, c), 1)),
            pl.BlockSpec((lv, D_INNER), lambda b, c: (rows(b, c), 2)),
            pl.BlockSpec((lv, D_INNER), lambda b, c: (rows(b, c), 3)),
            pl.BlockSpec((lv, 2 * LANES), lambda b, c: (rows(b, c), 0)),
            pl.BlockSpec((1, 2 * LANES), lambda b, c: (0, 0)),
            pl.BlockSpec((1, D_INNER), lambda b, c: (0, 0)),
            pl.BlockSpec((1, H_A, DV_A, DK_A), lambda b, c: (b, 0, 0, 0)),
            pl.BlockSpec((1, H_A, DK_A), lambda b, c: (b, 0, 0)),
            pl.BlockSpec((1, 1, LANES), lambda b, c: (b, 0, 0)),
        ],
        out_specs=[
            pl.BlockSpec((lv, D_INNER), lambda b, c: (rows(b, c), 0)),
            pl.BlockSpec((1, H_A, DV_A, DK_A), lambda b, c: (b, 0, 0, 0)),
            pl.BlockSpec((1, H_A, DK_A), lambda b, c: (b, 0, 0)),
            pl.BlockSpec((1, 1, LANES), lambda b, c: (b, 0, 0)),
        ],
        out_shape=[
            jax.ShapeDtypeStruct((m, D_INNER), act_dtype),
            jax.ShapeDtypeStruct((bsz, H_A, DV_A, DK_A), F32),
            jax.ShapeDtypeStruct((bsz, H_A, DK_A), F32),
            jax.ShapeDtypeStruct((bsz, 1, LANES), F32),
        ],
        scratch_shapes=[
            pltpu.VMEM((H_A, DK_A, DV_A), F32),
            pltpu.VMEM((H_A, DK_A), F32),
            pltpu.VMEM((1, LANES), F32),
        ],
        compiler_params=_cparams(2),
        name="mlstm_mixer",
    )(proj, proj, proj, proj, proj, gates, bias, hnorm, c0, n0, m0)


def _gated_conv(bg, cg, xin, z, cw, prev):
    u = cg * xin
    p2 = prev[0:1, :]
    p1 = prev[1:2, :]
    row = lax.broadcasted_iota(jnp.int32, u.shape, 0)
    u1 = jnp.where(row == 0, p1, pltpu.roll(u, 1, axis=0))
    u2 = jnp.where(row == 0, p2, jnp.where(row == 1, p1, pltpu.roll(u, 2, axis=0)))
    conv = cw[0:1, :] * u2 + cw[1:2, :] * u1 + cw[2:3, :] * u
    return bg * conv * _silu(z), u[u.shape[0] - (CONV_W - 1):, :]


def _conv_kernel(bg_ref, cg_ref, xin_ref, z_ref, cw_ref, st_ref, act_ref, st_out, carry):
    @pl.when(pl.program_id(1) == 0)
    def _():
        carry[...] = st_ref[0]

    y, last = _gated_conv(bg_ref[...].astype(F32), cg_ref[...].astype(F32), xin_ref[...].astype(F32),
                          z_ref[...].astype(F32), cw_ref[...], carry[...])
    act_ref[...] = y.astype(act_ref.dtype)
    carry[...] = last
    st_out[0] = last


def _conv(proj, conv_w, state, bsz, t, act_dtype):
    tt = min(256, t)
    nt = t // tt
    m = bsz * t
    blk = lambda j: pl.BlockSpec((tt, D_INNER), lambda b, i: (b * nt + i, j))
    return pl.pallas_call(
        _conv_kernel,
        grid=(bsz, nt),
        in_specs=[
            blk(0), blk(1), blk(2), blk(3),
            pl.BlockSpec((CONV_W, D_INNER), lambda b, i: (0, 0)),
            pl.BlockSpec((1, CONV_W - 1, D_INNER), lambda b, i: (b, 0, 0)),
        ],
        out_specs=[
            pl.BlockSpec((tt, D_INNER), lambda b, i: (b * nt + i, 0)),
            pl.BlockSpec((1, CONV_W - 1, D_INNER), lambda b, i: (b, 0, 0)),
        ],
        out_shape=[
            jax.ShapeDtypeStruct((m, D_INNER), act_dtype),
            jax.ShapeDtypeStruct((bsz, CONV_W - 1, D_INNER), F32),
        ],
        scratch_shapes=[pltpu.VMEM((CONV_W - 1, D_INNER), F32)],
        compiler_params=_cparams(2),
        name="conv_mixer",
    )(proj, proj, proj, proj, conv_w, state)


def _proj_conv_kernel(x_ref, g_ref, w_ref, cw_ref, st_ref, act_ref, st_out, h_scr, carry, *, tiles_per_seq):
    i = pl.program_id(0)
    j = pl.program_id(1)
    cs = act_ref.shape[1]

    @pl.when(j == 0)
    def _():
        h_scr[...] = _rms(x_ref[...], g_ref[...]).astype(BF16)

    res = jnp.dot(h_scr[...], w_ref[j], preferred_element_type=F32)
    prev = jnp.where(i % tiles_per_seq == 0, st_ref[0], carry[j])
    y, last = _gated_conv(res[:, :cs], res[:, cs:2 * cs], res[:, 2 * cs:3 * cs], res[:, 3 * cs:],
                          cw_ref[j], prev)
    act_ref[...] = y.astype(act_ref.dtype)
    carry[j] = last
    st_out[0] = last


def _proj_conv(x, gain, w, conv_w, state, bsz, t):
    m, d = x.shape
    tm = 1024
    assert t % tm == 0
    cs = PROJ_TN // 4
    nt = D_INNER // cs
    w_tiles = w.reshape(d, 4, nt, cs).transpose(2, 0, 1, 3).reshape(nt, d, 4 * cs).astype(BF16)
    cw_tiles = conv_w.reshape(CONV_W, nt, cs).transpose(1, 0, 2)
    tiles_per_seq = t // tm
    resident = lambda shape: pl.BlockSpec(shape, lambda i, j: (0,) * len(shape), pipeline_mode=pl.Buffered(1))
    state_spec = pl.BlockSpec((1, CONV_W - 1, cs), lambda i, j: (i // tiles_per_seq, 0, j))
    tile_state_spec = pl.BlockSpec((1, CONV_W - 1, cs), lambda i, j: (i, 0, j))
    act, tile_states = pl.pallas_call(
        functools.partial(_proj_conv_kernel, tiles_per_seq=tiles_per_seq),
        grid=(m // tm, nt),
        in_specs=[
            pl.BlockSpec((tm, d), lambda i, j: (i, 0)),
            pl.BlockSpec((1, d), lambda i, j: (0, 0)),
            resident((nt, d, 4 * cs)),
            resident((nt, CONV_W, cs)),
            state_spec,
        ],
        out_specs=[pl.BlockSpec((tm, cs), lambda i, j: (i, j)), tile_state_spec],
        out_shape=[
            jax.ShapeDtypeStruct((m, D_INNER), BF16),
            jax.ShapeDtypeStruct((m // tm, CONV_W - 1, D_INNER), F32),
        ],
        scratch_shapes=[pltpu.VMEM((tm, d), BF16), pltpu.VMEM((nt, CONV_W - 1, cs), F32)],
        compiler_params=_cparams(2),
        name="norm_proj_conv",
    )(x, gain, w_tiles, cw_tiles, state)
    return act, tile_states[tiles_per_seq - 1::tiles_per_seq]


AUX_K0 = 3 * H_C


def _fgate_kernel(g_ref, bias_ref, lf_ref, c_ref, *rest, lv, with_aux):
    carry = rest[-1]
    L = CHUNK

    @pl.when(pl.program_id(1) == 0)
    def _():
        carry[...] = jnp.zeros_like(carry)

    lf = _log_sigmoid(g_ref[...] + bias_ref[...])
    lf_ref[...] = lf
    lf = _pad_rows(lf, L)
    if lv < L:
        lf = jnp.where(lax.broadcasted_iota(jnp.int32, (L, LANES), 0) < lv, lf, 0.0)
    row = lax.broadcasted_iota(jnp.int32, (L, L), 0)
    col = lax.broadcasted_iota(jnp.int32, (L, L), 1)
    tri = jnp.where(col <= row, 1.0, 0.0).astype(BF16)
    cs = sum(jnp.dot(tri, piece, preferred_element_type=F32) for piece in _split3(lf)) + carry[...]
    c_ref[...] = cs[:lv]
    carry[...] = cs[L - 1:L, :]
    if with_aux:
        qa_ref, ka_ref = rest[0], rest[1]
        qa = jnp.zeros((L, LANES), F32)
        ka = jnp.zeros((L, LANES), F32)
        head_ok = row < H_C
        for j, piece in enumerate(_split3(cs * LOG2E)):
            to_q = jnp.where(head_ok & (col == 3 * row + j), 1.0, 0.0).astype(BF16)
            to_k = jnp.where(head_ok & (col == AUX_K0 + 3 * row + j), 1.0, 0.0).astype(BF16)
            qa = qa + jnp.dot(piece, to_q, preferred_element_type=F32)
            ka = ka - jnp.dot(piece, to_k, preferred_element_type=F32)
        qa_ref[...] = qa.astype(BF16)
        ka_ref[...] = ka.astype(BF16)


def _fgate(gates, bias, bsz, t, with_aux):
    lv = min(CHUNK, t)
    nc = t // lv
    m = bsz * t
    rows = pl.BlockSpec((lv, LANES), lambda b, c: (b * nc + c, 0))
    aux_spec = rows
    aux_shape = jax.ShapeDtypeStruct((m, LANES), BF16)
    kern = functools.partial(_fgate_kernel, lv=lv, with_aux=with_aux)
    return pl.pallas_call(
        kern,
        grid=(bsz, nc),
        in_specs=[rows, pl.BlockSpec((1, LANES), lambda b, c: (0, 0))],
        out_specs=[rows, rows] + [aux_spec] * (2 * with_aux),
        out_shape=[jax.ShapeDtypeStruct((m, LANES), F32)] * 2 + [aux_shape] * (2 * with_aux),
        scratch_shapes=[pltpu.VMEM((1, LANES), F32)],
        compiler_params=_cparams(2),
        name="fgate_scan",
    )(gates, bias)


def _online_softmax_step(s_blocks, m, l, acc, values):
    bmax = functools.reduce(jnp.maximum, s_blocks)
    m_new = jnp.maximum(m, jnp.max(bmax, axis=1, keepdims=True))
    corr = jnp.exp2(m - m_new)
    ps = [jnp.exp2(blk - m_new) for blk in s_blocks]
    l_new = None if l is None else l * corr + functools.reduce(jnp.add, ps)
    p = ps[0] if len(ps) == 1 else jnp.concatenate(ps, axis=1)
    corr_acc = jnp.concatenate([corr] * (acc.shape[1] // LANES), axis=1)
    acc_new = acc * corr_acc + jnp.dot(p.astype(BF16), values, preferred_element_type=F32)
    return m_new, l_new, acc_new


def _lane_blocks(s):
    return [s[:, b * LANES:(b + 1) * LANES] for b in range(s.shape[1] // LANES)]


FLASH_T = 256


def _flash_kernel(q_ref, k_ref, v_ref, z_ref, qa_ref, ka_ref, o_ref, *, tq):
    nq = q_ref.shape[0] // tq
    causal = (lax.broadcasted_iota(jnp.int32, (tq, tq), 1)
              <= lax.broadcasted_iota(jnp.int32, (tq, tq), 0))
    lane = lax.broadcasted_iota(jnp.int32, (1, LANES), 1) - 3 * pl.program_id(1)
    ones_q = jnp.where((lane >= AUX_K0) & (lane < AUX_K0 + 3), 1.0, 0.0).astype(BF16)
    ones_k = jnp.where((lane >= 0) & (lane < 3), 1.0, 0.0).astype(BF16)
    ones_v = jnp.ones((tq, LANES), BF16)
    for i in range(nq):
        rs = slice(i * tq, (i + 1) * tq)
        q_aug = jnp.concatenate([q_ref[rs, :], qa_ref[rs, :] + ones_q], axis=1)
        m = jnp.full((tq, LANES), -jnp.inf, F32)
        acc = jnp.zeros((tq, DH_C + LANES), F32)
        for j in range(i + 1):
            ks = slice(j * tq, (j + 1) * tq)
            k_aug = jnp.concatenate([k_ref[ks, :], ka_ref[ks, :] + ones_k], axis=1)
            s = lax.dot_general(q_aug, k_aug, (((1,), (1,)), ((), ())), preferred_element_type=F32)
            if j == i:
                s = jnp.where(causal, s, -jnp.inf)
            values = jnp.concatenate([v_ref[ks, :], ones_v], axis=1)
            m, _, acc = _online_softmax_step(_lane_blocks(s), m, None, acc, values)
        o = acc[:, :DH_C] * (1.0 / acc[:, DH_C:])
        o_ref[rs, :] = (o * _silu(z_ref[rs, :].astype(F32))).astype(o_ref.dtype)


def _flash(proj, q_aux, k_aux, bsz, t):
    tq = min(FLASH_T, t)
    col = lambda j: pl.BlockSpec((t, DH_C), lambda b, h: (b, j * H_C + h))
    aux = pl.BlockSpec((t, LANES), lambda b, h: (b, 0))
    return pl.pallas_call(
        functools.partial(_flash_kernel, tq=tq),
        grid=(bsz, H_C),
        in_specs=[col(0), col(1), col(2), col(3), aux, aux],
        out_specs=pl.BlockSpec((t, DH_C), lambda b, h: (b, h)),
        out_shape=jax.ShapeDtypeStruct((bsz * t, D_INNER), BF16),
        compiler_params=_cparams(2),
        name="fox_flash",
    )(proj, proj, proj, proj, q_aux, k_aux)


PAGES_PER_STEP = 8


def _paged_kernel(pt_ref, q_ref, kn_ref, vn_ref, z_ref, cqc_ref, cnf_ref, *rest, t):
    page_refs = rest[:3 * PAGES_PER_STEP]
    out_ref, qs_scr, wq_scr, mask_scr, spread_scr, m_scr, l_scr, acc_scr, carry = rest[3 * PAGES_PER_STEP:]
    p = pl.program_id(1)
    P = PAGE_SIZE
    R = H_C * t
    W = P * H_C
    t_shift = t.bit_length() - 1
    h_shift = H_C.bit_length() - 1

    @pl.when(p == 0)
    def _():
        m_scr[...] = jnp.full_like(m_scr, -jnp.inf)
        l_scr[...] = jnp.zeros_like(l_scr)
        acc_scr[...] = jnp.zeros_like(acc_scr)
        carry[...] = jnp.zeros_like(carry)
        q = q_ref[...] * (DH_C ** -0.5 * LOG2E)
        q_rows = jnp.concatenate([q[:, h * DH_C:(h + 1) * DH_C] for h in range(H_C)], axis=0)
        qs_scr[...] = q_rows.astype(BF16)
        q_t = q_rows.T.astype(BF16)
        zero = jnp.zeros_like(q_t)
        wq_scr[...] = jnp.concatenate([jnp.concatenate([q_t, zero], axis=1),
                                       jnp.concatenate([zero, q_t], axis=1)], axis=0)
        row_h = lax.shift_right_logical(lax.broadcasted_iota(jnp.int32, (R, W), 0), t_shift)
        col_h = lax.broadcasted_iota(jnp.int32, (R, W), 1) & (H_C - 1)
        mask_scr[...] = jnp.where(row_h == col_h, cqc_ref[0] * LOG2E, -jnp.inf)
        pos = lax.broadcasted_iota(jnp.int32, (P, W), 0)
        col_p = lax.shift_right_logical(lax.broadcasted_iota(jnp.int32, (P, W), 1), h_shift)
        spread_scr[...] = jnp.where(pos > col_p, 1.0, 0.0).astype(BF16)

    qs = qs_scr[...]
    head = lax.broadcasted_iota(jnp.int32, (H_C, W), 0)
    col_h16 = lax.broadcasted_iota(jnp.int32, (H_C, W), 1) & (H_C - 1)
    state = (m_scr[...], l_scr[...], acc_scr[...])
    suffix_carry = carry[...]
    pages = list(zip(*[page_refs[i::3] for i in range(3)]))
    pieces = jnp.concatenate([piece for _, _, lfp_ref in pages for piece in _split3(lfp_ref[0])], axis=0)
    g_all = jnp.dot(pieces, spread_scr[...], preferred_element_type=F32)
    half = W // 2
    for i, (kp_ref, vp_ref, lfp_ref) in enumerate(pages):
        g = sum(g_all[(3 * i + j) * H_C:(3 * i + j + 1) * H_C, :] for j in range(3)) + suffix_carry
        bias = jnp.sum(jnp.where(head == col_h16, g, 0.0), axis=0, keepdims=True) * LOG2E
        suffix_carry = suffix_carry + jnp.sum(lfp_ref[0], axis=1, keepdims=True)
        keys = jnp.concatenate([kp_ref[0, :half, :].astype(BF16), kp_ref[0, half:, :].astype(BF16)], axis=1)
        st = jnp.dot(keys, wq_scr[...], preferred_element_type=F32)
        s = jnp.concatenate([st[:, :R].T, st[:, R:].T], axis=1)
        state = _online_softmax_step(_lane_blocks(s + mask_scr[...] + bias), *state, vp_ref[0].astype(BF16))
    carry[...] = suffix_carry
    m_scr[...], l_scr[...], acc_scr[...] = state

    @pl.when(p == pl.num_programs(1) - 1)
    def _():
        sn = lax.dot_general(qs, kn_ref[0].astype(BF16), (((1,), (1,)), ((), ())), preferred_element_type=F32)
        row = lax.broadcasted_iota(jnp.int32, (R, R), 0)
        col = lax.broadcasted_iota(jnp.int32, (R, R), 1)
        valid = (((col & (H_C - 1)) == lax.shift_right_logical(row, t_shift))
                 & (lax.shift_right_logical(col, h_shift) <= (row & (t - 1))))
        sn = jnp.where(valid, sn + (cqc_ref[0] - cnf_ref[0]) * LOG2E, -jnp.inf)
        _, l, acc = _online_softmax_step([sn], m_scr[...], l_scr[...], acc_scr[...], vn_ref[0].astype(BF16))
        o = acc * (1.0 / jnp.sum(l, axis=1, keepdims=True))
        for h in range(H_C):
            zh = z_ref[:, h * DH_C:(h + 1) * DH_C]
            out_ref[:, h * DH_C:(h + 1) * DH_C] = (o[h * t:(h + 1) * t, :] * _silu(zh)).astype(out_ref.dtype)


def _paged(page_table, proj, kn_rows, vn_rows, cq_col, cn_flat, pool_k, pool_v, pool_lf_t, bsz, t):
    assert t & (t - 1) == 0 and t * H_C == LANES, "sample length must be a power of two with t*H == 128"
    n_pages = page_table.shape[1]
    g = PAGES_PER_STEP
    assert n_pages % g == 0
    R = H_C * t
    W = PAGE_SIZE * H_C
    per_b = lambda b, p, pt: (b, 0, 0)
    page_specs = []
    for i in range(g):
        page = lambda b, p, pt, i=i: (pt[b, n_pages - 1 - (p * g + i)], 0, 0)
        page_specs += [pl.BlockSpec((1, W, DH_C), page), pl.BlockSpec((1, W, DH_C), page),
                       pl.BlockSpec((1, H_C, PAGE_SIZE), page)]
    grid_spec = pltpu.PrefetchScalarGridSpec(
        num_scalar_prefetch=1,
        grid=(bsz, n_pages // g),
        in_specs=[
            pl.BlockSpec((t, D_INNER), lambda b, p, pt: (b, 0)),
            pl.BlockSpec((1, R, DH_C), per_b),
            pl.BlockSpec((1, R, DH_C), per_b),
            pl.BlockSpec((t, D_INNER), lambda b, p, pt: (b, 3)),
            pl.BlockSpec((1, R, 1), per_b),
            pl.BlockSpec((1, 1, R), per_b),
        ] + page_specs,
        out_specs=pl.BlockSpec((t, D_INNER), lambda b, p, pt: (b, 0)),
        scratch_shapes=[
            pltpu.VMEM((R, DH_C), BF16),
            pltpu.VMEM((2 * DH_C, 2 * R), BF16),
            pltpu.VMEM((R, W), F32),
            pltpu.VMEM((PAGE_SIZE, W), BF16),
            pltpu.VMEM((R, LANES), F32),
            pltpu.VMEM((R, LANES), F32),
            pltpu.VMEM((R, DH_C), F32),
            pltpu.VMEM((H_C, 1), F32),
        ],
    )
    return pl.pallas_call(
        functools.partial(_paged_kernel, t=t),
        grid_spec=grid_spec,
        out_shape=jax.ShapeDtypeStruct((bsz * t, D_INNER), F32),
        compiler_params=_cparams(2),
        name="fox_paged",
    )(page_table, proj, kn_rows, vn_rows, proj, cq_col, cn_flat, *([pool_k, pool_v, pool_lf_t] * g))


def _pad_cols(w, n):
    return jnp.pad(w, ((0, 0), (0, n - w.shape[1])))


def _lane_row(vec, offset=0, width=LANES):
    return jnp.zeros((1, width), F32).at[0, offset:offset + vec.shape[0]].set(vec)


def kernel(x_prompt, x_sample, state_mlstm_c, state_mlstm_n, state_mlstm_m, state_conv, cache_k, cache_v, cache_logf, page_table, p_prompt, p_sample, ln_gain, w_in_a, b_igate_a, b_fgate_a, hnorm_a, w_out_a, w_in_b, conv_w_b, w_out_b, w_in_c, b_fgate_c, w_out_c, w_pe, w_pg, pe_norm, final_norm):
    bp, tp, _ = x_prompt.shape
    bs, ts, _ = x_sample.shape
    xp = x_prompt.reshape(bp * tp, D_MODEL)
    xs = x_sample.reshape(bs * ts, D_MODEL)
    final_gain = final_norm.reshape(1, D_MODEL)
    qk = H_A * DK_A
    g0 = 2 * qk + D_INNER
    f0 = 3 * D_INNER

    outs = {name: [] for name in (
        "mc_p", "mn_p", "mm_p", "mc_s", "mn_s", "mm_s", "cv_p", "cv_s",
        "k_p", "v_p", "f_p", "k_s", "v_s", "f_s")}
    yp = ys = None
    for i in range(DEPTH):
        j = i // N_MIXERS
        gain = ln_gain[i].reshape(1, D_MODEL)
        kind = i % N_MIXERS
        if kind == 0:
            w = w_in_a[j]
            w_main = _tile_cols(jnp.concatenate([w[:, :g0], w[:, g0 + 2 * H_A:]], axis=1))
            w_gate = jnp.concatenate([_pad_cols(w[:, g0:g0 + H_A], LANES),
                                      _pad_cols(w[:, g0 + H_A:g0 + 2 * H_A], LANES)], axis=1).astype(BF16)
            bias = jnp.concatenate([_lane_row(b_igate_a[j]), _lane_row(b_fgate_a[j])], axis=1)
            hn = hnorm_a[j].reshape(1, D_INNER)
            w_out = w_out_a[j]

            proj_p, gates_p = _proj(xp, gain, w_main, w_gate, BF16)
            act_p, c_p, n_p, m_p = _mlstm(
                proj_p, gates_p, bias, hn,
                jnp.zeros((bp, H_A, DV_A, DK_A), F32), jnp.zeros((bp, H_A, DK_A), F32),
                jnp.zeros((bp, 1, LANES), F32), bp, tp, BF16)
            proj_s, gates_s = _proj(xs, gain, w_main, w_gate, F32)
            m0 = jnp.pad(state_mlstm_m[j], ((0, 0), (0, LANES - H_A))).reshape(bs, 1, LANES)
            act_s, c_s, n_s, m_s = _mlstm(
                proj_s, gates_s, bias, hn, state_mlstm_c[j], state_mlstm_n[j], m0, bs, ts, F32)
            outs["mc_p"].append(c_p); outs["mn_p"].append(n_p); outs["mm_p"].append(m_p[:, 0, :H_A])
            outs["mc_s"].append(c_s); outs["mn_s"].append(n_s); outs["mm_s"].append(m_s[:, 0, :H_A])
        elif kind == 1:
            w_main = _tile_cols(w_in_b[j])
            w_gate = jnp.zeros((D_MODEL, LANES), BF16)
            w_out = w_out_b[j]
            act_p, st_p = _proj_conv(xp, gain, w_in_b[j], conv_w_b[j],
                                     jnp.zeros((bp, CONV_W - 1, D_INNER), F32), bp, tp)
            proj_s, _ = _proj(xs, gain, w_main, w_gate, F32)
            act_s, st_s = _conv(proj_s, conv_w_b[j], state_conv[j], bs, ts, F32)
            outs["cv_p"].append(st_p); outs["cv_s"].append(st_s)
        else:
            w = w_in_c[j]
            w_main = _tile_cols(jnp.concatenate([w[:, :f0], w[:, f0 + H_C:]], axis=1))
            w_gate = _pad_cols(w[:, f0:f0 + H_C], LANES).astype(BF16)
            bias = _lane_row(b_fgate_c[j])
            w_out = w_out_c[j]

            q_scale = jnp.ones((4 * D_INNER,), F32).at[:D_INNER].set(DH_C ** -0.5 * LOG2E)
            kt = D_INNER // PROJ_TN
            proj_p, gates_p, k_p, v_p = _proj(xp, gain, w_main, w_gate, BF16, col_scale=q_scale,
                                              f32_tiles=((kt, 2 * kt), (2 * kt, 3 * kt)))
            lf_p, _, q_aux, k_aux = _fgate(gates_p, bias, bp, tp, True)
            act_p = _flash(proj_p, q_aux, k_aux, bp, tp)
            outs["k_p"].append(k_p.reshape(bp, tp, H_C, DH_C))
            outs["v_p"].append(v_p.reshape(bp, tp, H_C, DH_C))
            outs["f_p"].append(lf_p[:, :H_C].reshape(bp, tp, H_C))

            proj_s, gates_s = _proj(xs, gain, w_main, w_gate, F32)
            lf_s, c_s = _fgate(gates_s, bias, bs, ts, False)
            k_s = proj_s[:, D_INNER:2 * D_INNER]
            v_s = proj_s[:, 2 * D_INNER:3 * D_INNER]
            c_new = c_s.reshape(bs, ts, LANES)[:, :, :H_C]
            cq_col = c_new.transpose(0, 2, 1).reshape(bs, H_C * ts, 1)
            cn_flat = c_new.reshape(bs, 1, ts * H_C)
            n_pool = cache_k.shape[1]
            pool_k = cache_k[j].reshape(n_pool, PAGE_SIZE * H_C, DH_C)
            pool_v = cache_v[j].reshape(n_pool, PAGE_SIZE * H_C, DH_C)
            pool_lf_t = cache_logf[j].transpose(0, 2, 1)
            act_s = _paged(page_table, proj_s, k_s.reshape(bs, ts * H_C, DH_C), v_s.reshape(bs, ts * H_C, DH_C),
                           cq_col, cn_flat, pool_k, pool_v, pool_lf_t, bs, ts)
            outs["k_s"].append(k_s.reshape(bs, ts, H_C, DH_C))
            outs["v_s"].append(v_s.reshape(bs, ts, H_C, DH_C))
            outs["f_s"].append(lf_s[:, :H_C].reshape(bs, ts, H_C))

        final = i == DEPTH - 1
        w_out = w_out.astype(BF16)
        wpe = w_pe[i].astype(BF16)
        wpg = w_pg[i].astype(BF16)
        pg = pe_norm[i].reshape(1, D_MODEL)
        xp, yp = _out_embed(act_p, xp, p_prompt[i].reshape(bp * tp, PE_DIM), w_out, wpe, wpg, pg, final_gain, final)
        xs, ys = _out_embed(act_s, xs, p_sample[i].reshape(bs * ts, PE_DIM), w_out, wpe, wpg, pg, final_gain, final)

    st = lambda name: jnp.stack(outs[name]) if len(outs[name]) > 1 else outs[name][0][None]
    return (yp.reshape(bp, tp, D_MODEL), ys.reshape(bs, ts, D_MODEL),
            st("mc_p"), st("mn_p"), st("mm_p"), st("mc_s"), st("mn_s"), st("mm_s"),
            st("cv_p"), st("cv_s"),
            st("k_p"), st("v_p"), st("f_p"), st("k_s"), st("v_s"), st("f_s"))
```

```python
import functools

import jax
import jax.numpy as jnp
from jax import lax
from jax.experimental import pallas as pl
from jax.experimental.pallas import tpu as pltpu

F32 = jnp.float32
BF16 = jnp.bfloat16

EPS = 1e-6
DEPTH = 4
N_MIXERS = 3
D_MODEL = 1024
D_INNER = 2 * D_MODEL
H_A = 8
DK_A = D_INNER // (2 * H_A)
DV_A = D_INNER // H_A
CONV_W = 3
DH_C = 128
H_C = D_INNER // DH_C
PE_DIM = 256
PAGE_SIZE = 128
LOG2E = 1.4426950408889634

LANES = 128
SUBLANES = 8
CHUNK = 128
VMEM_LIMIT_BYTES = 48 * 1024 * 1024


def _cparams(n_axes):
    return pltpu.CompilerParams(
        dimension_semantics=("arbitrary",) * n_axes,
        vmem_limit_bytes=VMEM_LIMIT_BYTES)


def _log1p(u):
    w = 1.0 + u
    return jnp.where(w == 1.0, u, jnp.log(w) * (u / (w - 1.0)))


def _log_sigmoid(x):
    return jnp.minimum(x, 0.0) - _log1p(jnp.exp(-jnp.abs(x)))


def _silu(x):
    return x * jax.nn.sigmoid(x)


def _rms(x, gain):
    return x * lax.rsqrt(jnp.mean(x * x, axis=-1, keepdims=True) + EPS) * gain


def _split3(x):
    hi = x.astype(BF16)
    r1 = x - hi.astype(F32)
    mid = r1.astype(BF16)
    lo = (r1 - mid.astype(F32)).astype(BF16)
    return hi, mid, lo


def _pad_rows(x, rows):
    if x.shape[0] == rows:
        return x
    return jnp.concatenate([x, jnp.zeros((rows - x.shape[0], x.shape[1]), x.dtype)], axis=0)


PROJ_TN = 2048


def _proj_kernel(x_ref, g_ref, w_ref, wg_ref, cs_ref, o_ref, og_ref, *rest, f32_tiles, scaled, tn):
    extra_refs, h_scr = rest[:-1], rest[-1]
    j = pl.program_id(1)

    @pl.when(j == 0)
    def _():
        h = _rms(x_ref[...], g_ref[...]).astype(BF16)
        h_scr[...] = h
        og_ref[...] = jnp.dot(h, wg_ref[...], preferred_element_type=F32)

    res = jnp.dot(h_scr[...], w_ref[:, pl.ds(pl.multiple_of(j * tn, tn), tn)], preferred_element_type=F32)
    if scaled:
        res = res * cs_ref[j]
    o_ref[...] = res.astype(o_ref.dtype)
    for ref, (lo, hi) in zip(extra_refs, f32_tiles):
        @pl.when((j >= lo) & (j < hi))
        def _(ref=ref):
            ref[...] = res


def _proj(x, gain, w_tiles, wg, out_dtype, col_scale=None, f32_tiles=()):
    m, d = x.shape
    tn = PROJ_TN
    nt = w_tiles.shape[1] // tn
    ng = wg.shape[1]
    tm = min(512 if f32_tiles else 1024, m)
    scaled = col_scale is not None
    cs = (col_scale if scaled else jnp.ones((nt * tn,), F32)).reshape(nt, 1, tn)
    resident = lambda shape: pl.BlockSpec(shape, lambda i, j: (0,) * len(shape), pipeline_mode=pl.Buffered(1))
    extra_specs = [pl.BlockSpec((tm, tn), lambda i, j, lo=lo, hi=hi: (i, jnp.clip(j - lo, 0, hi - lo - 1)))
                   for lo, hi in f32_tiles]
    extra_shapes = [jax.ShapeDtypeStruct((m, (hi - lo) * tn), F32) for lo, hi in f32_tiles]
    kern = functools.partial(_proj_kernel, f32_tiles=tuple(f32_tiles), scaled=scaled, tn=tn)
    return pl.pallas_call(
        kern,
        grid=(m // tm, nt),
        in_specs=[
            pl.BlockSpec((tm, d), lambda i, j: (i, 0)),
            pl.BlockSpec((1, d), lambda i, j: (0, 0)),
            resident((d, nt * tn)),
            resident((d, ng)),
            resident((nt, 1, tn)),
        ],
        out_specs=[
            pl.BlockSpec((tm, tn), lambda i, j: (i, j)),
            pl.BlockSpec((tm, ng), lambda i, j: (i, 0)),
        ] + extra_specs,
        out_shape=[
            jax.ShapeDtypeStruct((m, nt * tn), out_dtype),
            jax.ShapeDtypeStruct((m, ng), F32),
        ] + extra_shapes,
        scratch_shapes=[pltpu.VMEM((tm, d), BF16)],
        compiler_params=_cparams(2),
        name="norm_proj",
    )(x, gain, w_tiles, wg, cs)


def _tile_cols(w):
    return w.astype(BF16)


def _out_embed_kernel(act_ref, x_ref, p_ref, wo_ref, wpe_ref, wpg_ref, g_ref, fg_ref,
                      xo_ref, *maybe_y_ref):
    y = jnp.dot(act_ref[...].astype(BF16), wo_ref[...], preferred_element_type=F32)
    x1 = x_ref[...] + y
    rn = _rms(x1, g_ref[...]).astype(BF16)
    gate = jax.nn.sigmoid(jnp.dot(rn, wpg_ref[...], preferred_element_type=F32))
    pe = jnp.dot(p_ref[...].astype(BF16), wpe_ref[...], preferred_element_type=F32)
    x2 = x1 + pe * gate
    xo_ref[...] = x2
    if maybe_y_ref:
        maybe_y_ref[0][...] = _rms(x2, fg_ref[...])


def _out_embed(act, x, p, layer, wo, wpe, wpg, gain, final_gain, final):
    m, d = x.shape
    di = act.shape[1]
    pe = p.shape[2]
    tm = min(1024, m)
    row = lambda i: (i, 0)
    const = lambda i: (0, 0)
    resident = lambda shape: pl.BlockSpec(shape, const, pipeline_mode=pl.Buffered(1))
    n_out = 2 if final else 1
    outs = pl.pallas_call(
        _out_embed_kernel,
        grid=(m // tm,),
        in_specs=[
            pl.BlockSpec((tm, di), row),
            pl.BlockSpec((tm, d), row),
            pl.BlockSpec((None, tm, pe), lambda i: (layer, i, 0)),
            resident((di, d)),
            resident((pe, d)),
            resident((d, d)),
            resident((1, d)),
            resident((1, d)),
        ],
        out_specs=[pl.BlockSpec((tm, d), row)] * n_out,
        out_shape=[jax.ShapeDtypeStruct((m, d), F32)] * n_out,
        compiler_params=_cparams(1),
        name="out_embed",
    )(act, x, p, wo, wpe, wpg, gain, final_gain)
    return outs if final else (outs[0], None)


def _mlstm_kernel(q_ref, k_ref, v_ref, o_ref, z_ref, g_ref, bias_ref, hn_ref,
                  c0_ref, n0_ref, m0_ref,
                  act_ref, c_out, n_out, m_out,
                  ct_scr, n_scr, m_scr, *, lv):
    c = pl.program_id(1)
    L = CHUNK
    scale = DK_A ** -0.5

    @pl.when(c == 0)
    def _():
        for h in range(H_A):
            ct_scr[h] = c0_ref[0, h].T
        n_scr[...] = n0_ref[0]
        m_scr[...] = m0_ref[0]

    lq = max(lv, 2 * SUBLANES)
    q = _pad_rows(q_ref[...], lq).astype(BF16)
    k = _pad_rows(k_ref[...], L).astype(BF16)
    v = _pad_rows(v_ref[...], L).astype(BF16)
    g = _pad_rows(g_ref[...], L) + bias_ref[...]
    li = g[:, :LANES]
    lf = _log_sigmoid(g[:, LANES:])
    if lv < L:
        valid = lax.broadcasted_iota(jnp.int32, (L, LANES), 0) < lv
        li = jnp.where(valid, li, -jnp.inf)
        lf = jnp.where(valid, lf, 0.0)

    row = lax.broadcasted_iota(jnp.int32, (L, L), 0)
    col = lax.broadcasted_iota(jnp.int32, (L, L), 1)
    causal = col <= row
    causal_q = causal[:lq]
    tri = jnp.where(causal, 1.0, 0.0).astype(BF16)
    b = sum(jnp.dot(tri, piece, preferred_element_type=F32) for piece in _split3(lf))
    r = li - b
    r_t = r.T
    m_row = m_scr[...]
    b_last = b[L - 1:L, :]
    m_new = jnp.maximum(b_last + m_row, jnp.max(b_last + r, axis=0, keepdims=True))
    decay = jnp.exp(b_last + m_row - m_new)
    ws_t = jnp.exp(b_last + r - m_new).T
    n_all = n_scr[...]
    ones_bf16 = jnp.ones((L, LANES), BF16)

    for h in range(H_A):
        qh = q[:, h * DK_A:(h + 1) * DK_A]
        kh = k[:, h * DK_A:(h + 1) * DK_A]
        vh = v[:, h * DV_A:(h + 1) * DV_A]
        wide = lambda x: jnp.concatenate([x] * (DV_A // LANES), axis=1)
        m0h = m_row[:, h:h + 1]
        rmat = jnp.where(causal_q, r_t[h:h + 1, :], -jnp.inf)
        mx = jnp.broadcast_to(jnp.maximum(jnp.max(rmat, axis=1, keepdims=True), m0h), (lq, LANES))
        w_intra = jnp.exp(rmat - mx)
        w_inter = jnp.exp(m0h - mx) * scale
        qk = lax.dot_general(qh, kh, (((1,), (1,)), ((), ())), preferred_element_type=F32)
        s = qk * scale * w_intra
        s_hi = s.astype(BF16)
        s_lo = (s - s_hi.astype(F32)).astype(BF16)
        ct = ct_scr[h]
        num = (jnp.dot(s_hi, vh, preferred_element_type=F32)
               + wide(w_inter) * jnp.dot(qh, ct.astype(BF16), preferred_element_type=F32))
        n_rows = jnp.broadcast_to(n_all[h:h + 1, :], (LANES, DK_A)).astype(BF16)
        qn = lax.dot_general(qh, n_rows, (((1,), (1,)), ((), ())), preferred_element_type=F32)
        row_sum = (jnp.dot(s_hi, ones_bf16, preferred_element_type=F32)
                   + jnp.dot(s_lo, ones_bf16, preferred_element_type=F32))
        den = row_sum + w_inter * qn
        floor = jnp.exp(-(b[:lq, h:h + 1] + mx))
        hh = num * wide(1.0 / jnp.maximum(jnp.abs(den), floor))
        msq = jnp.broadcast_to(jnp.mean(hh * hh, axis=1, keepdims=True), (lq, LANES))
        hh = hh * wide(lax.rsqrt(msq + EPS))
        hh = hh * hn_ref[:, h * DV_A:(h + 1) * DV_A]
        oh = o_ref[:, h * DV_A:(h + 1) * DV_A].astype(F32)
        zh = z_ref[:, h * DV_A:(h + 1) * DV_A].astype(F32)
        y = hh[:lv] * zh * (1.0 / ((1.0 + jnp.exp(-oh)) * (1.0 + jnp.exp(-zh))))
        act_ref[:, h * DV_A:(h + 1) * DV_A] = y.astype(act_ref.dtype)

        ws_row = ws_t[h:h + 1, :]
        kts = (kh.astype(F32).T * ws_row).astype(BF16)
        dh = decay[:, h:h + 1]
        ct_scr[h] = dh * ct + jnp.dot(kts, vh, preferred_element_type=F32)
        ws8 = jnp.broadcast_to(ws_row, (SUBLANES, L)).astype(BF16)
        n_scr[h:h + 1, :] = dh * n_all[h:h + 1, :] + jnp.dot(ws8, kh, preferred_element_type=F32)[0:1]
    m_scr[...] = m_new

    @pl.when(c == pl.num_programs(1) - 1)
    def _():
        for h in range(H_A):
            c_out[0, h] = ct_scr[h].T
        n_out[0] = n_scr[...]
        m_out[0] = m_scr[...]


def _mlstm(proj, gates, bias, hnorm, c0, n0, layer, m0, bsz, t, act_dtype):
    lv = min(CHUNK, t)
    nc = t // lv
    m = bsz * t
    rows = lambda b, c: b * nc + c
    kern = functools.partial(_mlstm_kernel, lv=lv)
    return pl.pallas_call(
        kern,
        grid=(bsz, nc),
        in_specs=[
            pl.BlockSpec((lv, H_A * DK_A), lambda b, c: (rows(b, c), 0)),
            pl.BlockSpec((lv, H_A * DK_A), lambda b, c: (rows(b, c), 1)),
            pl.BlockSpec((lv, D_INNER), lambda b, c: (rows(b, c), 1)),
            pl.BlockSpec((lv, D_INNER), lambda b, c: (rows(b, c), 2)),
            pl.BlockSpec((lv, D_INNER), lambda b, c: (rows(b, c), 3)),
            pl.BlockSpec((lv, 2 * LANES), lambda b, c: (rows(b, c), 0)),
            pl.BlockSpec((1, 2 * LANES), lambda b, c: (0, 0)),
            pl.BlockSpec((1, D_INNER), lambda b, c: (0, 0)),
            pl.BlockSpec((None, 1, H_A, DV_A, DK_A), lambda b, c: (layer, b, 0, 0, 0)),
            pl.BlockSpec((None, 1, H_A, DK_A), lambda b, c: (layer, b, 0, 0)),
            pl.BlockSpec((1, 1, LANES), lambda b, c: (b, 0, 0)),
        ],
        out_specs=[
            pl.BlockSpec((lv, D_INNER), lambda b, c: (rows(b, c), 0)),
            pl.BlockSpec((1, H_A, DV_A, DK_A), lambda b, c: (b, 0, 0, 0)),
            pl.BlockSpec((1, H_A, DK_A), lambda b, c: (b, 0, 0)),
            pl.BlockSpec((1, 1, LANES), lambda b, c: (b, 0, 0)),
        ],
        out_shape=[
            jax.ShapeDtypeStruct((m, D_INNER), act_dtype),
            jax.ShapeDtypeStruct((bsz, H_A, DV_A, DK_A), F32),
            jax.ShapeDtypeStruct((bsz, H_A, DK_A), F32),
            jax.ShapeDtypeStruct((bsz, 1, LANES), F32),
        ],
        scratch_shapes=[
            pltpu.VMEM((H_A, DK_A, DV_A), F32),
            pltpu.VMEM((H_A, DK_A), F32),
            pltpu.VMEM((1, LANES), F32),
        ],
        compiler_params=_cparams(2),
        name="mlstm_mixer",
    )(proj, proj, proj, proj, proj, gates, bias, hnorm, c0, n0, m0)


def _gated_conv(bg, cg, xin, z, cw, prev):
    u = cg * xin
    p2 = prev[0:1, :]
    p1 = prev[1:2, :]
    row = lax.broadcasted_iota(jnp.int32, u.shape, 0)
    u1 = jnp.where(row == 0, p1, pltpu.roll(u, 1, axis=0))
    u2 = jnp.where(row == 0, p2, jnp.where(row == 1, p1, pltpu.roll(u, 2, axis=0)))
    conv = cw[0:1, :] * u2 + cw[1:2, :] * u1 + cw[2:3, :] * u
    return bg * conv * _silu(z), u[u.shape[0] - (CONV_W - 1):, :]


def _conv_kernel(bg_ref, cg_ref, xin_ref, z_ref, cw_ref, st_ref, act_ref, st_out, carry):
    @pl.when(pl.program_id(1) == 0)
    def _():
        carry[...] = st_ref[0]

    y, last = _gated_conv(bg_ref[...].astype(F32), cg_ref[...].astype(F32), xin_ref[...].astype(F32),
                          z_ref[...].astype(F32), cw_ref[...], carry[...])
    act_ref[...] = y.astype(act_ref.dtype)
    carry[...] = last
    st_out[0] = last


def _conv(proj, conv_w, state, bsz, t, act_dtype):
    tt = min(256, t)
    nt = t // tt
    m = bsz * t
    blk = lambda j: pl.BlockSpec((tt, D_INNER), lambda b, i: (b * nt + i, j))
    return pl.pallas_call(
        _conv_kernel,
        grid=(bsz, nt),
        in_specs=[
            blk(0), blk(1), blk(2), blk(3),
            pl.BlockSpec((CONV_W, D_INNER), lambda b, i: (0, 0)),
            pl.BlockSpec((1, CONV_W - 1, D_INNER), lambda b, i: (b, 0, 0)),
        ],
        out_specs=[
            pl.BlockSpec((tt, D_INNER), lambda b, i: (b * nt + i, 0)),
            pl.BlockSpec((1, CONV_W - 1, D_INNER), lambda b, i: (b, 0, 0)),
        ],
        out_shape=[
            jax.ShapeDtypeStruct((m, D_INNER), act_dtype),
            jax.ShapeDtypeStruct((bsz, CONV_W - 1, D_INNER), F32),
        ],
        scratch_shapes=[pltpu.VMEM((CONV_W - 1, D_INNER), F32)],
        compiler_params=_cparams(2),
        name="conv_mixer",
    )(proj, proj, proj, proj, conv_w, state)


def _proj_conv_kernel(x_ref, g_ref, w_ref, cw_ref, st_ref, act_ref, st_out, h_scr, carry, *, tiles_per_seq):
    i = pl.program_id(0)
    j = pl.program_id(1)
    cs = act_ref.shape[1]

    @pl.when(j == 0)
    def _():
        h_scr[...] = _rms(x_ref[...], g_ref[...]).astype(BF16)

    h = h_scr[...]
    bg, cg, xin, z = (
        jnp.dot(h, w_ref[:, pl.ds(pl.multiple_of(g * D_INNER + j * cs, cs), cs)], preferred_element_type=F32)
        for g in range(4))
    prev = jnp.where(i % tiles_per_seq == 0, st_ref[0], carry[j])
    y, last = _gated_conv(bg, cg, xin, z, cw_ref[:, pl.ds(pl.multiple_of(j * cs, cs), cs)], prev)
    act_ref[...] = y.astype(act_ref.dtype)
    carry[j] = last
    st_out[0] = last


def _proj_conv(x, gain, w, conv_w, state, bsz, t):
    m, d = x.shape
    tm = 1024
    assert t % tm == 0
    cs = PROJ_TN // 4
    nt = D_INNER // cs
    tiles_per_seq = t // tm
    resident = lambda shape: pl.BlockSpec(shape, lambda i, j: (0,) * len(shape), pipeline_mode=pl.Buffered(1))
    state_spec = pl.BlockSpec((1, CONV_W - 1, cs), lambda i, j: (i // tiles_per_seq, 0, j))
    tile_state_spec = pl.BlockSpec((1, CONV_W - 1, cs), lambda i, j: (i, 0, j))
    act, tile_states = pl.pallas_call(
        functools.partial(_proj_conv_kernel, tiles_per_seq=tiles_per_seq),
        grid=(m // tm, nt),
        in_specs=[
            pl.BlockSpec((tm, d), lambda i, j: (i, 0)),
            pl.BlockSpec((1, d), lambda i, j: (0, 0)),
            resident((d, 4 * D_INNER)),
            resident((CONV_W, D_INNER)),
            state_spec,
        ],
        out_specs=[pl.BlockSpec((tm, cs), lambda i, j: (i, j)), tile_state_spec],
        out_shape=[
            jax.ShapeDtypeStruct((m, D_INNER), BF16),
            jax.ShapeDtypeStruct((m // tm, CONV_W - 1, D_INNER), F32),
        ],
        scratch_shapes=[pltpu.VMEM((tm, d), BF16), pltpu.VMEM((nt, CONV_W - 1, cs), F32)],
        compiler_params=_cparams(2),
        name="norm_proj_conv",
    )(x, gain, w, conv_w, state)
    return act, tile_states[tiles_per_seq - 1::tiles_per_seq]


AUX_K0 = 3 * H_C


def _fgate_kernel(g_ref, bias_ref, lf_ref, c_ref, *rest, lv, with_aux):
    carry = rest[-1]
    L = CHUNK

    @pl.when(pl.program_id(1) == 0)
    def _():
        carry[...] = jnp.zeros_like(carry)

    lf = _log_sigmoid(g_ref[...] + bias_ref[...])
    lf_ref[...] = lf
    lf = _pad_rows(lf, L)
    if lv < L:
        lf = jnp.where(lax.broadcasted_iota(jnp.int32, (L, LANES), 0) < lv, lf, 0.0)
    row = lax.broadcasted_iota(jnp.int32, (L, L), 0)
    col = lax.broadcasted_iota(jnp.int32, (L, L), 1)
    tri = jnp.where(col <= row, 1.0, 0.0).astype(BF16)
    cs = sum(jnp.dot(tri, piece, preferred_element_type=F32) for piece in _split3(lf)) + carry[...]
    c_ref[...] = cs[:lv]
    carry[...] = cs[L - 1:L, :]
    if with_aux:
        qa_ref, ka_ref = rest[0], rest[1]
        qa = jnp.zeros((L, LANES), F32)
        ka = jnp.zeros((L, LANES), F32)
        head_ok = row < H_C
        for j, piece in enumerate(_split3(cs * LOG2E)):
            to_q = jnp.where(head_ok & (col == 3 * row + j), 1.0, 0.0).astype(BF16)
            to_k = jnp.where(head_ok & (col == AUX_K0 + 3 * row + j), 1.0, 0.0).astype(BF16)
            qa = qa + jnp.dot(piece, to_q, preferred_element_type=F32)
            ka = ka - jnp.dot(piece, to_k, preferred_element_type=F32)
        qa_ref[...] = qa.astype(BF16)
        ka_ref[...] = ka.astype(BF16)


def _fgate(gates, bias, bsz, t, with_aux):
    lv = min(CHUNK, t)
    nc = t // lv
    m = bsz * t
    rows = pl.BlockSpec((lv, LANES), lambda b, c: (b * nc + c, 0))
    aux_spec = rows
    aux_shape = jax.ShapeDtypeStruct((m, LANES), BF16)
    kern = functools.partial(_fgate_kernel, lv=lv, with_aux=with_aux)
    return pl.pallas_call(
        kern,
        grid=(bsz, nc),
        in_specs=[rows, pl.BlockSpec((1, LANES), lambda b, c: (0, 0))],
        out_specs=[rows, rows] + [aux_spec] * (2 * with_aux),
        out_shape=[jax.ShapeDtypeStruct((m, LANES), F32)] * 2 + [aux_shape] * (2 * with_aux),
        scratch_shapes=[pltpu.VMEM((1, LANES), F32)],
        compiler_params=_cparams(2),
        name="fgate_scan",
    )(gates, bias)


def _online_softmax_step(s_blocks, m, l, acc, values):
    bmax = functools.reduce(jnp.maximum, s_blocks)
    m_new = jnp.maximum(m, jnp.max(bmax, axis=1, keepdims=True))
    corr = jnp.exp2(m - m_new)
    ps = [jnp.exp2(blk - m_new) for blk in s_blocks]
    l_new = None if l is None else l * corr + functools.reduce(jnp.add, ps)
    p = ps[0] if len(ps) == 1 else jnp.concatenate(ps, axis=1)
    corr_acc = jnp.concatenate([corr] * (acc.shape[1] // LANES), axis=1)
    acc_new = acc * corr_acc + jnp.dot(p.astype(BF16), values, preferred_element_type=F32)
    return m_new, l_new, acc_new


def _lane_blocks(s):
    return [s[:, b * LANES:(b + 1) * LANES] for b in range(s.shape[1] // LANES)]


FLASH_T = 256


def _flash_kernel(q_ref, k_ref, v_ref, z_ref, qa_ref, ka_ref, o_ref, *, tq):
    nq = q_ref.shape[0] // tq
    causal = (lax.broadcasted_iota(jnp.int32, (tq, tq), 1)
              <= lax.broadcasted_iota(jnp.int32, (tq, tq), 0))
    lane = lax.broadcasted_iota(jnp.int32, (1, LANES), 1) - 3 * pl.program_id(1)
    ones_q = jnp.where((lane >= AUX_K0) & (lane < AUX_K0 + 3), 1.0, 0.0).astype(BF16)
    ones_k = jnp.where((lane >= 0) & (lane < 3), 1.0, 0.0).astype(BF16)
    ones_v = jnp.ones((tq, LANES), BF16)
    for i in range(nq):
        rs = slice(i * tq, (i + 1) * tq)
        q_aug = jnp.concatenate([q_ref[rs, :], qa_ref[rs, :] + ones_q], axis=1)
        m = jnp.full((tq, LANES), -jnp.inf, F32)
        acc = jnp.zeros((tq, DH_C + LANES), F32)
        for j in range(i + 1):
            ks = slice(j * tq, (j + 1) * tq)
            k_aug = jnp.concatenate([k_ref[ks, :], ka_ref[ks, :] + ones_k], axis=1)
            s = lax.dot_general(q_aug, k_aug, (((1,), (1,)), ((), ())), preferred_element_type=F32)
            if j == i:
                s = jnp.where(causal, s, -jnp.inf)
            values = jnp.concatenate([v_ref[ks, :], ones_v], axis=1)
            m, _, acc = _online_softmax_step(_lane_blocks(s), m, None, acc, values)
        o = acc[:, :DH_C] * (1.0 / acc[:, DH_C:])
        o_ref[rs, :] = (o * _silu(z_ref[rs, :].astype(F32))).astype(o_ref.dtype)


def _flash(proj, q_aux, k_aux, bsz, t):
    tq = min(FLASH_T, t)
    col = lambda j: pl.BlockSpec((t, DH_C), lambda b, h: (b, j * H_C + h))
    aux = pl.BlockSpec((t, LANES), lambda b, h: (b, 0))
    return pl.pallas_call(
        functools.partial(_flash_kernel, tq=tq),
        grid=(bsz, H_C),
        in_specs=[col(0), col(1), col(2), col(3), aux, aux],
        out_specs=pl.BlockSpec((t, DH_C), lambda b, h: (b, h)),
        out_shape=jax.ShapeDtypeStruct((bsz * t, D_INNER), BF16),
        compiler_params=_cparams(2),
        name="fox_flash",
    )(proj, proj, proj, proj, q_aux, k_aux)


PAGES_PER_STEP = 8


def _paged_kernel(pt_ref, q_ref, kn_ref, vn_ref, z_ref, cqc_ref, cnf_ref, *rest, t):
    page_refs = rest[:3 * PAGES_PER_STEP]
    out_ref, qs_scr, wq_scr, mask_scr, spread_scr, m_scr, l_scr, acc_scr, carry = rest[3 * PAGES_PER_STEP:]
    p = pl.program_id(1)
    P = PAGE_SIZE
    R = H_C * t
    W = P * H_C
    t_shift = t.bit_length() - 1
    h_shift = H_C.bit_length() - 1

    @pl.when(p == 0)
    def _():
        m_scr[...] = jnp.full_like(m_scr, -jnp.inf)
        l_scr[...] = jnp.zeros_like(l_scr)
        acc_scr[...] = jnp.zeros_like(acc_scr)
        carry[...] = jnp.zeros_like(carry)
        q = q_ref[...] * (DH_C ** -0.5 * LOG2E)
        q_rows = jnp.concatenate([q[:, h * DH_C:(h + 1) * DH_C] for h in range(H_C)], axis=0)
        qs_scr[...] = q_rows.astype(BF16)
        q_t = q_rows.T.astype(BF16)
        zero = jnp.zeros_like(q_t)
        wq_scr[...] = jnp.concatenate([jnp.concatenate([q_t, zero], axis=1),
                                       jnp.concatenate([zero, q_t], axis=1)], axis=0)
        row_h = lax.shift_right_logical(lax.broadcasted_iota(jnp.int32, (R, W), 0), t_shift)
        col_h = lax.broadcasted_iota(jnp.int32, (R, W), 1) & (H_C - 1)
        mask_scr[...] = jnp.where(row_h == col_h, cqc_ref[0] * LOG2E, -jnp.inf)
        pos = lax.broadcasted_iota(jnp.int32, (P, W), 0)
        col_p = lax.shift_right_logical(lax.broadcasted_iota(jnp.int32, (P, W), 1), h_shift)
        spread_scr[...] = jnp.where(pos > col_p, 1.0, 0.0).astype(BF16)

    qs = qs_scr[...]
    head = lax.broadcasted_iota(jnp.int32, (H_C, W), 0)
    col_h16 = lax.broadcasted_iota(jnp.int32, (H_C, W), 1) & (H_C - 1)
    state = (m_scr[...], l_scr[...], acc_scr[...])
    suffix_carry = carry[...]
    pages = list(zip(*[page_refs[i::3] for i in range(3)]))
    pieces = jnp.concatenate([piece for _, _, lfp_ref in pages for piece in _split3(lfp_ref[0])], axis=0)
    g_all = jnp.dot(pieces, spread_scr[...], preferred_element_type=F32)
    half = W // 2
    for i, (kp_ref, vp_ref, lfp_ref) in enumerate(pages):
        g = sum(g_all[(3 * i + j) * H_C:(3 * i + j + 1) * H_C, :] for j in range(3)) + suffix_carry
        bias = jnp.sum(jnp.where(head == col_h16, g, 0.0), axis=0, keepdims=True) * LOG2E
        suffix_carry = suffix_carry + jnp.sum(lfp_ref[0], axis=1, keepdims=True)
        keys = jnp.concatenate([kp_ref[0, :half, :].astype(BF16), kp_ref[0, half:, :].astype(BF16)], axis=1)
        st = jnp.dot(keys, wq_scr[...], preferred_element_type=F32)
        s = jnp.concatenate([st[:, :R].T, st[:, R:].T], axis=1)
        state = _online_softmax_step(_lane_blocks(s + mask_scr[...] + bias), *state, vp_ref[0].astype(BF16))
    carry[...] = suffix_carry
    m_scr[...], l_scr[...], acc_scr[...] = state

    @pl.when(p == pl.num_programs(1) - 1)
    def _():
        sn = lax.dot_general(qs, kn_ref[0].astype(BF16), (((1,), (1,)), ((), ())), preferred_element_type=F32)
        row = lax.broadcasted_iota(jnp.int32, (R, R), 0)
        col = lax.broadcasted_iota(jnp.int32, (R, R), 1)
        valid = (((col & (H_C - 1)) == lax.shift_right_logical(row, t_shift))
                 & (lax.shift_right_logical(col, h_shift) <= (row & (t - 1))))
        sn = jnp.where(valid, sn + (cqc_ref[0] - cnf_ref[0]) * LOG2E, -jnp.inf)
        _, l, acc = _online_softmax_step([sn], m_scr[...], l_scr[...], acc_scr[...], vn_ref[0].astype(BF16))
        o = acc * (1.0 / jnp.sum(l, axis=1, keepdims=True))
        for h in range(H_C):
            zh = z_ref[:, h * DH_C:(h + 1) * DH_C]
            out_ref[:, h * DH_C:(h + 1) * DH_C] = (o[h * t:(h + 1) * t, :] * _silu(zh)).astype(out_ref.dtype)


def _paged(page_table, proj, kn_rows, vn_rows, cq_col, cn_flat, pool_k, pool_v, pool_lf_t, bsz, t):
    assert t & (t - 1) == 0 and t * H_C == LANES, "sample length must be a power of two with t*H == 128"
    n_pages = page_table.shape[1]
    g = PAGES_PER_STEP
    assert n_pages % g == 0
    R = H_C * t
    W = PAGE_SIZE * H_C
    per_b = lambda b, p, pt: (b, 0, 0)
    page_specs = []
    for i in range(g):
        page = lambda b, p, pt, i=i: (pt[b, n_pages - 1 - (p * g + i)], 0, 0)
        page_specs += [pl.BlockSpec((1, W, DH_C), page), pl.BlockSpec((1, W, DH_C), page),
                       pl.BlockSpec((1, H_C, PAGE_SIZE), page)]
    grid_spec = pltpu.PrefetchScalarGridSpec(
        num_scalar_prefetch=1,
        grid=(bsz, n_pages // g),
        in_specs=[
            pl.BlockSpec((t, D_INNER), lambda b, p, pt: (b, 0)),
            pl.BlockSpec((1, R, DH_C), per_b),
            pl.BlockSpec((1, R, DH_C), per_b),
            pl.BlockSpec((t, D_INNER), lambda b, p, pt: (b, 3)),
            pl.BlockSpec((1, R, 1), per_b),
            pl.BlockSpec((1, 1, R), per_b),
        ] + page_specs,
        out_specs=pl.BlockSpec((t, D_INNER), lambda b, p, pt: (b, 0)),
        scratch_shapes=[
            pltpu.VMEM((R, DH_C), BF16),
            pltpu.VMEM((2 * DH_C, 2 * R), BF16),
            pltpu.VMEM((R, W), F32),
            pltpu.VMEM((PAGE_SIZE, W), BF16),
            pltpu.VMEM((R, LANES), F32),
            pltpu.VMEM((R, LANES), F32),
            pltpu.VMEM((R, DH_C), F32),
            pltpu.VMEM((H_C, 1), F32),
        ],
    )
    return pl.pallas_call(
        functools.partial(_paged_kernel, t=t),
        grid_spec=grid_spec,
        out_shape=jax.ShapeDtypeStruct((bsz * t, D_INNER), F32),
        compiler_params=_cparams(2),
        name="fox_paged",
    )(page_table, proj, kn_rows, vn_rows, proj, cq_col, cn_flat, *([pool_k, pool_v, pool_lf_t] * g))


def _pad_cols(w, n):
    return jnp.pad(w, ((0, 0), (0, n - w.shape[1])))


def _lane_row(vec, offset=0, width=LANES):
    return jnp.zeros((1, width), F32).at[0, offset:offset + vec.shape[0]].set(vec)


def kernel(x_prompt, x_sample, state_mlstm_c, state_mlstm_n, state_mlstm_m, state_conv, cache_k, cache_v, cache_logf, page_table, p_prompt, p_sample, ln_gain, w_in_a, b_igate_a, b_fgate_a, hnorm_a, w_out_a, w_in_b, conv_w_b, w_out_b, w_in_c, b_fgate_c, w_out_c, w_pe, w_pg, pe_norm, final_norm):
    bp, tp, _ = x_prompt.shape
    bs, ts, _ = x_sample.shape
    xp = x_prompt.reshape(bp * tp, D_MODEL)
    xs = x_sample.reshape(bs * ts, D_MODEL)
    final_gain = final_norm.reshape(1, D_MODEL)
    pp_all = p_prompt.reshape(DEPTH, bp * tp, PE_DIM)
    ps_all = p_sample.reshape(DEPTH, bs * ts, PE_DIM)
    qk = H_A * DK_A
    g0 = 2 * qk + D_INNER
    f0 = 3 * D_INNER

    outs = {name: [] for name in (
        "mc_p", "mn_p", "mm_p", "mc_s", "mn_s", "mm_s", "cv_p", "cv_s",
        "k_p", "v_p", "f_p", "k_s", "v_s", "f_s")}
    yp = ys = None
    for i in range(DEPTH):
        j = i // N_MIXERS
        gain = ln_gain[i].reshape(1, D_MODEL)
        kind = i % N_MIXERS
        if kind == 0:
            w = w_in_a[j]
            w_main = _tile_cols(jnp.concatenate([w[:, :g0], w[:, g0 + 2 * H_A:]], axis=1))
            w_gate = jnp.concatenate([_pad_cols(w[:, g0:g0 + H_A], LANES),
                                      _pad_cols(w[:, g0 + H_A:g0 + 2 * H_A], LANES)], axis=1).astype(BF16)
            bias = jnp.concatenate([_lane_row(b_igate_a[j]), _lane_row(b_fgate_a[j])], axis=1)
            hn = hnorm_a[j].reshape(1, D_INNER)
            w_out = w_out_a[j]

            proj_p, gates_p = _proj(xp, gain, w_main, w_gate, BF16)
            act_p, c_p, n_p, m_p = _mlstm(
                proj_p, gates_p, bias, hn,
                jnp.zeros((1, bp, H_A, DV_A, DK_A), F32), jnp.zeros((1, bp, H_A, DK_A), F32), 0,
                jnp.zeros((bp, 1, LANES), F32), bp, tp, BF16)
            proj_s, gates_s = _proj(xs, gain, w_main, w_gate, F32)
            m0 = jnp.pad(state_mlstm_m[j], ((0, 0), (0, LANES - H_A))).reshape(bs, 1, LANES)
            act_s, c_s, n_s, m_s = _mlstm(
                proj_s, gates_s, bias, hn, state_mlstm_c, state_mlstm_n, j, m0, bs, ts, F32)
            outs["mc_p"].append(c_p); outs["mn_p"].append(n_p); outs["mm_p"].append(m_p[:, 0, :H_A])
            outs["mc_s"].append(c_s); outs["mn_s"].append(n_s); outs["mm_s"].append(m_s[:, 0, :H_A])
        elif kind == 1:
            w_main = _tile_cols(w_in_b[j])
            w_gate = jnp.zeros((D_MODEL, LANES), BF16)
            w_out = w_out_b[j]
            act_p, st_p = _proj_conv(xp, gain, w_main, conv_w_b[j],
                                     jnp.zeros((bp, CONV_W - 1, D_INNER), F32), bp, tp)
            proj_s, _ = _proj(xs, gain, w_main, w_gate, F32)
            act_s, st_s = _conv(proj_s, conv_w_b[j], state_conv[j], bs, ts, F32)
            outs["cv_p"].append(st_p); outs["cv_s"].append(st_s)
        else:
            w = w_in_c[j]
            w_main = _tile_cols(jnp.concatenate([w[:, :f0], w[:, f0 + H_C:]], axis=1))
            w_gate = _pad_cols(w[:, f0:f0 + H_C], LANES).astype(BF16)
            bias = _lane_row(b_fgate_c[j])
            w_out = w_out_c[j]

            q_scale = jnp.ones((4 * D_INNER,), F32).at[:D_INNER].set(DH_C ** -0.5 * LOG2E)
            kt = D_INNER // PROJ_TN
            proj_p, gates_p, k_p, v_p = _proj(xp, gain, w_main, w_gate, BF16, col_scale=q_scale,
                                              f32_tiles=((kt, 2 * kt), (2 * kt, 3 * kt)))
            lf_p, _, q_aux, k_aux = _fgate(gates_p, bias, bp, tp, True)
            act_p = _flash(proj_p, q_aux, k_aux, bp, tp)
            outs["k_p"].append(k_p.reshape(bp, tp, H_C, DH_C))
            outs["v_p"].append(v_p.reshape(bp, tp, H_C, DH_C))
            outs["f_p"].append(lf_p[:, :H_C].reshape(bp, tp, H_C))

            proj_s, gates_s = _proj(xs, gain, w_main, w_gate, F32)
            lf_s, c_s = _fgate(gates_s, bias, bs, ts, False)
            k_s = proj_s[:, D_INNER:2 * D_INNER]
            v_s = proj_s[:, 2 * D_INNER:3 * D_INNER]
            c_new = c_s.reshape(bs, ts, LANES)[:, :, :H_C]
            cq_col = c_new.transpose(0, 2, 1).reshape(bs, H_C * ts, 1)
            cn_flat = c_new.reshape(bs, 1, ts * H_C)
            n_pool = cache_k.shape[1]
            pool_k = cache_k[j].reshape(n_pool, PAGE_SIZE * H_C, DH_C)
            pool_v = cache_v[j].reshape(n_pool, PAGE_SIZE * H_C, DH_C)
            pool_lf_t = cache_logf[j].transpose(0, 2, 1)
            act_s = _paged(page_table, proj_s, k_s.reshape(bs, ts * H_C, DH_C), v_s.reshape(bs, ts * H_C, DH_C),
                           cq_col, cn_flat, pool_k, pool_v, pool_lf_t, bs, ts)
            outs["k_s"].append(k_s.reshape(bs, ts, H_C, DH_C))
            outs["v_s"].append(v_s.reshape(bs, ts, H_C, DH_C))
            outs["f_s"].append(lf_s[:, :H_C].reshape(bs, ts, H_C))

        final = i == DEPTH - 1
        w_out = w_out.astype(BF16)
        wpe = w_pe[i].astype(BF16)
        wpg = w_pg[i].astype(BF16)
        pg = pe_norm[i].reshape(1, D_MODEL)
        xp, yp = _out_embed(act_p, xp, pp_all, i, w_out, wpe, wpg, pg, final_gain, final)
        xs, ys = _out_embed(act_s, xs, ps_all, i, w_out, wpe, wpg, pg, final_gain, final)

    st = lambda name: jnp.stack(outs[name]) if len(outs[name]) > 1 else outs[name][0][None]
    return (yp.reshape(bp, tp, D_MODEL), ys.reshape(bs, ts, D_MODEL),
            st("mc_p"), st("mn_p"), st("mm_p"), st("mc_s"), st("mn_s"), st("mm_s"),
            st("cv_p"), st("cv_s"),
            st("k_p"), st("v_p"), st("f_p"), st("k_s"), st("v_s"), st("f_s"))
```
